```python
import jax, jax.numpy as jnp
from jax import lax
import numpy as np

D_MODEL = 1024
BATCH = 8
SEQ = 4096
DEPTH = 4

N_HEADS = 8
HEAD_DIM = 64
W_ATTN = N_HEADS * HEAD_DIM
C_CONF = D_MODEL // 2
CONF_WIDTH = 31
C_SC = D_MODEL // 2
SC_WIDTH = 3
N_BRANCH = 3
D_FF = ((8 * D_MODEL + 2) // 3 + 255) // 256 * 256
BLOCK_Q = 128
EPS = 1e-6
SPLIT_SIZES = (W_ATTN, W_ATTN, W_ATTN, C_CONF, C_CONF, C_SC, C_SC, C_SC, N_BRANCH * D_MODEL)
SPLIT_IDX = tuple(int(i) for i in np.cumsum(SPLIT_SIZES)[:-1])
N_IN = int(sum(SPLIT_SIZES))

kernel_name = "hybrid_stickbreak_conformer_shortconv_gated"


def rms_norm(x, g):
    xf = x.astype(jnp.float32)
    y = xf * lax.rsqrt(jnp.mean(xf * xf, axis=-1, keepdims=True) + EPS)
    return (y * g.astype(jnp.float32)).astype(x.dtype)


def layer_norm(x, g, b):
    xf = x.astype(jnp.float32)
    mu = jnp.mean(xf, axis=-1, keepdims=True)
    var = jnp.mean(jnp.square(xf - mu), axis=-1, keepdims=True)
    y = (xf - mu) * lax.rsqrt(var + EPS)
    return (y * g.astype(jnp.float32) + b.astype(jnp.float32)).astype(x.dtype)


def causal_depthwise_conv(x, w):
    width, ch = w.shape
    return lax.conv_general_dilated(
        x, w[:, None, :].astype(x.dtype), window_strides=(1,), padding=[(width - 1, 0)],
        dimension_numbers=("NWC", "WIO", "NWC"), feature_group_count=ch)


def stick_breaking_attention(q, k, v):
    b, h, s, dh = q.shape
    nb = s // BLOCK_Q
    scale = float(1.0 / np.sqrt(dh))
    qb = q.reshape(b, h, nb, BLOCK_Q, dh).transpose(2, 0, 1, 3, 4)
    kpos = jnp.arange(s)

    def one_block(args):
        qi, bi = args
        z = jnp.einsum("bhqd,bhkd->bhqk", qi, k, preferred_element_type=jnp.float32) * scale
        qpos = bi * BLOCK_Q + jnp.arange(BLOCK_Q)
        causal = kpos[None, :] < qpos[:, None]
        log_1m_beta = jnp.where(causal, -jax.nn.softplus(z), 0.0)
        rest = lax.cumsum(log_1m_beta, axis=log_1m_beta.ndim - 1, reverse=True) - log_1m_beta
        attn = jnp.where(causal, jnp.exp(jax.nn.log_sigmoid(z) + rest), 0.0)
        return jnp.einsum("bhqk,bhkd->bhqd", attn.astype(v.dtype), v)

    out = lax.map(one_block, (qb, jnp.arange(nb)))
    return out.transpose(1, 2, 0, 3, 4).reshape(b, h, s, dh)


def setup_inputs(seed: int = 0) -> dict:
    key = jax.random.key(seed)
    ks = jax.random.split(key, 20)
    f32 = jnp.float32

    def nrm(k, shape, fan_in):
        return jax.random.normal(k, shape, f32) * (fan_in ** -0.5)

    def gain(k, shape):
        return 1.0 + 0.05 * jax.random.normal(k, shape, f32)

    def small(k, shape):
        return 0.02 * jax.random.normal(k, shape, f32)

    L = DEPTH
    return {
        "x": jax.random.normal(ks[0], (BATCH, SEQ, D_MODEL), f32),
        "mix_norm_g": gain(ks[1], (L, D_MODEL)),
        "w_in": nrm(ks[2], (L, D_MODEL, N_IN), D_MODEL),
        "q_norm_g": gain(ks[3], (L, HEAD_DIM)),
        "k_norm_g": gain(ks[4], (L, HEAD_DIM)),
        "w_attn_out": nrm(ks[5], (L, W_ATTN, D_MODEL), W_ATTN),
        "conf_dw_w": nrm(ks[6], (L, CONF_WIDTH, C_CONF), CONF_WIDTH),
        "conf_dw_b": small(ks[7], (L, C_CONF)),
        "conf_ln_g": gain(ks[8], (L, C_CONF)),
        "conf_ln_b": small(ks[9], (L, C_CONF)),
        "w_conf_out": nrm(ks[10], (L, C_CONF, D_MODEL), C_CONF),
        "sc_conv_w": nrm(ks[11], (L, SC_WIDTH, C_SC), SC_WIDTH),
        "w_sc_out": nrm(ks[12], (L, C_SC, D_MODEL), C_SC),
        "gate_b": small(ks[13], (L, N_BRANCH, D_MODEL)),
        "w_o": nrm(ks[14], (L, D_MODEL, D_MODEL), D_MODEL),
        "ffn_norm_g": gain(ks[15], (L, D_MODEL)),
        "w_ffn_in": nrm(ks[16], (L, D_MODEL, 2 * D_FF), D_MODEL),
        "w_ffn_out": nrm(ks[17], (L, D_FF, D_MODEL), D_FF),
    }


def reference(x, mix_norm_g, w_in, q_norm_g, k_norm_g, w_attn_out, conf_dw_w, conf_dw_b,
              conf_ln_g, conf_ln_b, w_conf_out, sc_conv_w, w_sc_out, gate_b, w_o,
              ffn_norm_g, w_ffn_in, w_ffn_out):
    b, s, _ = x.shape
    for l in range(DEPTH):
        u = rms_norm(x, mix_norm_g[l])
        proj = u @ w_in[l]
        q, k, v, cf_val, cf_gate, sc_x, sc_bg, sc_cg, gates = jnp.split(proj, SPLIT_IDX, axis=-1)

        q = rms_norm(q.reshape(b, s, N_HEADS, HEAD_DIM), q_norm_g[l]).transpose(0, 2, 1, 3)
        k = rms_norm(k.reshape(b, s, N_HEADS, HEAD_DIM), k_norm_g[l]).transpose(0, 2, 1, 3)
        v = v.reshape(b, s, N_HEADS, HEAD_DIM).transpose(0, 2, 1, 3)
        o = stick_breaking_attention(q, k, v).transpose(0, 2, 1, 3).reshape(b, s, W_ATTN)
        y_a = o @ w_attn_out[l]

        hb = cf_val * jax.nn.sigmoid(cf_gate)
        hb = causal_depthwise_conv(hb, conf_dw_w[l]) + conf_dw_b[l]
        hb = jax.nn.silu(layer_norm(hb, conf_ln_g[l], conf_ln_b[l]))
        y_b = hb @ w_conf_out[l]

        hc = sc_bg * causal_depthwise_conv(sc_cg * sc_x, sc_conv_w[l])
        y_c = hc @ w_sc_out[l]

        g = jax.nn.sigmoid(gates.reshape(b, s, N_BRANCH, D_MODEL) + gate_b[l])
        merged = g[:, :, 0] * y_a + g[:, :, 1] * y_b + g[:, :, 2] * y_c
        x = x + merged @ w_o[l]

        hf = rms_norm(x, ffn_norm_g[l]) @ w_ffn_in[l]
        gt, up = jnp.split(hf, 2, axis=-1)
        x = x + (jax.nn.silu(gt) * up) @ w_ffn_out[l]
    return x
```

```python
import functools

import jax
import jax.numpy as jnp
import numpy as np
from jax import lax
from jax.experimental import pallas as pl
from jax.experimental.pallas import tpu as pltpu

F32 = jnp.float32
BF16 = jnp.bfloat16

EPS = 1e-6
N_HEADS = 8
HEAD_DIM = 64
CONF_WIDTH = 31
SC_WIDTH = 3
LANES = 128
SUBLANES = 8
CHUNK = 512
CONF_HALO = 32
SC_HALO = 16
ATTN_BLOCK = 256
UNDERFLOW_BOUND = 104.0
VMEM_LIMIT_BYTES = 56 * 1024 * 1024


def _const_spec(shape):
    nd = len(shape)
    return pl.BlockSpec(shape, lambda *_: (0,) * nd, pipeline_mode=pl.Buffered(1))


def _sigmoid(x):
    return 1.0 / (1.0 + jnp.exp(-x))


def _in_proj_kernel(x_ref, ng_ref, w_ref, hmean_ref, qg_ref, kg_ref, gb_ref,
                    q_out, k_out, v_out, hb_out, m_out, bg_out, g_out):
    x = x_ref[...]
    ms = jnp.mean(x * x, axis=-1, keepdims=True)
    u = (x * lax.rsqrt(ms + EPS) * ng_ref[...]).astype(BF16)

    def proj(c):
        return jnp.dot(u, w_ref[:, c * CHUNK:(c + 1) * CHUNK], preferred_element_type=F32)

    def head_norm(a, g_ref):
        msq = jnp.dot((a * a).astype(BF16), hmean_ref[...], preferred_element_type=F32)
        return a * lax.rsqrt(msq + EPS) * g_ref[...]

    q_out[...] = head_norm(proj(0), qg_ref).astype(BF16)
    k_out[...] = head_norm(proj(1), kg_ref).astype(BF16)
    v_out[...] = proj(2).astype(BF16)
    cf_val = proj(3)
    hb_out[...] = (cf_val * _sigmoid(proj(4))).astype(hb_out.dtype)
    sc_x = proj(5)
    bg_out[...] = proj(6).astype(bg_out.dtype)
    m_out[...] = (proj(7) * sc_x).astype(m_out.dtype)
    n_gate_chunks = g_out.shape[-1] // CHUNK
    for j in range(n_gate_chunks):
        sl = slice(j * CHUNK, (j + 1) * CHUNK)
        g_out[:, sl] = _sigmoid(proj(8 + j) + gb_ref[:, sl]).astype(g_out.dtype)


def _in_proj(x2, ng, w, hmean, qg, kg, gb, *, tm, act_dtype):
    t, d = x2.shape
    n_in = w.shape[1]
    n_gate = n_in - 8 * CHUNK
    row = lambda i: (i, 0)
    outs = [
        jax.ShapeDtypeStruct((t, CHUNK), BF16),
        jax.ShapeDtypeStruct((t, CHUNK), BF16),
        jax.ShapeDtypeStruct((t, CHUNK), BF16),
        jax.ShapeDtypeStruct((t, CHUNK), act_dtype),
        jax.ShapeDtypeStruct((t, CHUNK), act_dtype),
        jax.ShapeDtypeStruct((t, CHUNK), act_dtype),
        jax.ShapeDtypeStruct((t, n_gate), BF16),
    ]
    out_specs = [pl.BlockSpec((tm, s.shape[1]), row) for s in outs]
    return pl.pallas_call(
        _in_proj_kernel,
        out_shape=outs,
        grid=(t // tm,),
        in_specs=[
            pl.BlockSpec((tm, d), row),
            _const_spec((1, d)),
            _const_spec((d, n_in)),
            _const_spec((CHUNK, CHUNK)),
            _const_spec((1, CHUNK)),
            _const_spec((1, CHUNK)),
            _const_spec((1, n_gate)),
        ],
        out_specs=out_specs,
        compiler_params=pltpu.CompilerParams(
            dimension_semantics=("arbitrary",), vmem_limit_bytes=VMEM_LIMIT_BYTES),
        name="in_proj",
    )(x2, ng, w, hmean, qg, kg, gb)


def _attn_kernel(q_ref, k_ref, v_ref, o_ref, c_scr, acc_scr):
    s_len = q_ref.shape[0]
    blk = ATTN_BLOCK
    n_blocks = s_len // blk
    lane = lax.broadcasted_iota(jnp.int32, (1, LANES), 1)
    head_masks = [lane < HEAD_DIM, lane >= HEAD_DIM]
    rows = lax.broadcasted_iota(jnp.int32, (blk, blk), 0)
    cols = lax.broadcasted_iota(jnp.int32, (blk, blk), 1)
    tri = (rows >= cols).astype(BF16)
    causal = cols < rows

    def key_block(qh, h, k0, masked):
        kblk = k_ref[pl.ds(k0, blk), :]
        vblk = jnp.where(head_masks[h], v_ref[pl.ds(k0, blk), :], jnp.zeros((), BF16))
        z = lax.dot_general(qh, kblk, (((1,), (1,)), ((), ())), preferred_element_type=F32)
        sp = jnp.maximum(z, 0.0) + jnp.log(1.0 + jnp.exp(-jnp.abs(z)))
        if masked:
            sp = jnp.where(causal, sp, 0.0)
        hi = sp.astype(BF16)
        lo = (sp - hi.astype(F32)).astype(BF16)
        cs = (jnp.dot(hi, tri, preferred_element_type=F32)
              + jnp.dot(lo, tri, preferred_element_type=F32))
        c_prev = c_scr[h]
        p = jnp.exp(z - cs - c_prev)
        if masked:
            p = jnp.where(causal, p, 0.0)
        c_new = c_prev + cs[:, 0:1]
        c_scr[h] = c_new
        acc_scr[...] += jnp.dot(p.astype(BF16), vblk, preferred_element_type=F32)
        return jnp.min(c_new)

    def q_block(qi, carry):
        q0 = pl.multiple_of(qi * blk, blk)
        qb = q_ref[pl.ds(q0, blk), :]
        qhs = [jnp.where(m, qb, jnp.zeros((), BF16)) for m in head_masks]
        c_scr[...] = jnp.zeros_like(c_scr)
        acc_scr[...] = jnp.zeros_like(acc_scr)
        cmin = jnp.minimum(key_block(qhs[0], 0, q0, True), key_block(qhs[1], 1, q0, True))

        def cond(st):
            kb, cmin = st
            return jnp.logical_and(kb >= 0, cmin < UNDERFLOW_BOUND)

        def body(st):
            kb, _ = st
            k0 = pl.multiple_of(kb * blk, blk)
            cmin = jnp.minimum(key_block(qhs[0], 0, k0, False), key_block(qhs[1], 1, k0, False))
            return kb - 1, cmin

        lax.while_loop(cond, body, (qi - 1, cmin))
        o_ref[pl.ds(q0, blk), :] = acc_scr[...].astype(o_ref.dtype)
        return carry

    lax.fori_loop(0, n_blocks, q_block, 0)


def _attention(q, k, v):
    b, s, w = q.shape
    n_pairs = w // LANES
    spec = pl.BlockSpec((None, s, LANES), lambda bi, hp: (bi, 0, hp))
    return pl.pallas_call(
        _attn_kernel,
        out_shape=jax.ShapeDtypeStruct((b, s, w), BF16),
        grid=(b, n_pairs),
        in_specs=[spec, spec, spec],
        out_specs=spec,
        scratch_shapes=[
            pltpu.VMEM((2, ATTN_BLOCK, 1), F32),
            pltpu.VMEM((ATTN_BLOCK, LANES), F32),
        ],
        compiler_params=pltpu.CompilerParams(
            dimension_semantics=("arbitrary", "arbitrary"), vmem_limit_bytes=VMEM_LIMIT_BYTES),
        name="stickbreak_attn",
    )(q, k, v)


def _post_kernel(x_ref, o_ref, hb_ref, hbh_ref, m_ref, mh_ref, bg_ref, g_ref,
                 cw_ref, cb_ref, lng_ref, lnb_ref, scw_ref,
                 wa_ref, wb_ref, wc_ref, wo_ref, fg_ref, w1_ref, w2_ref,
                 out_ref, hbe_scr, me_scr, hbn_scr, hc_scr, *, rows_per_step, ffn_chunk):
    i = pl.program_id(1)
    tm = x_ref.shape[0]
    d = x_ref.shape[1]
    has_left = (i > 0).astype(F32)

    hbe_scr[0:CONF_HALO, :] = hbh_ref[...].astype(F32) * has_left
    hbe_scr[CONF_HALO:, :] = hb_ref[...].astype(F32)
    me_scr[0:SC_HALO, :] = mh_ref[...].astype(F32) * has_left
    me_scr[SC_HALO:, :] = m_ref[...].astype(F32)

    rs = rows_per_step

    def causal_conv(ext_ref, halo, r0, w_ref, width):
        back = -(-(width - 1) // SUBLANES) * SUBLANES
        win = ext_ref[pl.ds(r0 + halo - back, rs + back), :]
        acc = jnp.zeros((rs, CHUNK), F32)
        for bb in range(min(SUBLANES, width)):
            rolled = win if bb == 0 else pltpu.roll(win, bb, axis=0)
            for aa in range((width - 1 - bb) // SUBLANES + 1):
                dd = SUBLANES * aa + bb
                kk = width - 1 - dd
                lo = back - SUBLANES * aa
                acc = acc + w_ref[kk:kk + 1, :] * rolled[lo:lo + rs, :]
        return acc

    def conv_rows(r, carry):
        r0 = pl.multiple_of(r * rs, rs)
        acc = causal_conv(hbe_scr, CONF_HALO, r0, cw_ref, CONF_WIDTH) + cb_ref[...]
        mu = jnp.mean(acc, axis=-1, keepdims=True)
        cen = acc - mu
        var = jnp.mean(cen * cen, axis=-1, keepdims=True)
        y = cen * lax.rsqrt(var + EPS) * lng_ref[...] + lnb_ref[...]
        hbn_scr[pl.ds(r0, rs), :] = (y * _sigmoid(y)).astype(BF16)

        acc2 = causal_conv(me_scr, SC_HALO, r0, scw_ref, SC_WIDTH)
        hc = bg_ref[pl.ds(r0, rs), :].astype(F32) * acc2
        hc_scr[pl.ds(r0, rs), :] = hc.astype(BF16)
        return carry

    lax.fori_loop(0, tm // rs, conv_rows, 0)

    y_a = jnp.dot(o_ref[...], wa_ref[...], preferred_element_type=F32)
    y_b = jnp.dot(hbn_scr[...], wb_ref[...], preferred_element_type=F32)
    y_c = jnp.dot(hc_scr[...], wc_ref[...], preferred_element_type=F32)
    merged = (g_ref[:, 0:d].astype(F32) * y_a
              + g_ref[:, d:2 * d].astype(F32) * y_b
              + g_ref[:, 2 * d:3 * d].astype(F32) * y_c)
    x1 = x_ref[...] + jnp.dot(merged.astype(BF16), wo_ref[...], preferred_element_type=F32)

    ms = jnp.mean(x1 * x1, axis=-1, keepdims=True)
    xn = (x1 * lax.rsqrt(ms + EPS) * fg_ref[...]).astype(BF16)
    fc = ffn_chunk
    n_chunks = w2_ref.shape[0] // fc
    y = x1
    for c in range(n_chunks):
        hf = jnp.dot(xn, w1_ref[:, 2 * fc * c:2 * fc * (c + 1)], preferred_element_type=F32)
        gt = hf[:, :fc]
        up = hf[:, fc:]
        act = (gt * _sigmoid(gt) * up).astype(BF16)
        y = y + jnp.dot(act, w2_ref[fc * c:fc * (c + 1), :], preferred_element_type=F32)
    out_ref[...] = y


def _post(x, o, hb, m, bg, g, cw, cb, lng, lnb, scw, wa, wb, wc, wo, fg, w1, w2,
          *, tm, ffn_chunk):
    b, s, d = x.shape
    n_gate = g.shape[-1]
    d_ff = w2.shape[0]
    tile = lambda width: pl.BlockSpec((None, tm, width), lambda bi, i: (bi, i, 0))

    def halo(rows):
        per_tile = tm // rows
        return pl.BlockSpec((None, rows, CHUNK),
                            lambda bi, i: (bi, jnp.maximum(i * per_tile - 1, 0), 0))

    kern = functools.partial(_post_kernel, rows_per_step=32, ffn_chunk=ffn_chunk)
    return pl.pallas_call(
        kern,
        out_shape=jax.ShapeDtypeStruct((b, s, d), F32),
        grid=(b, s // tm),
        in_specs=[
            tile(d), tile(CHUNK), tile(CHUNK), halo(CONF_HALO), tile(CHUNK), halo(SC_HALO),
            tile(CHUNK), tile(n_gate),
            _const_spec((CONF_WIDTH, CHUNK)), _const_spec((1, CHUNK)), _const_spec((1, CHUNK)),
            _const_spec((1, CHUNK)), _const_spec((SC_WIDTH, CHUNK)),
            _const_spec((CHUNK, d)), _const_spec((CHUNK, d)), _const_spec((CHUNK, d)),
            _const_spec((d, d)), _const_spec((1, d)),
            _const_spec((d, 2 * d_ff)), _const_spec((d_ff, d)),
        ],
        out_specs=tile(d),
        scratch_shapes=[
            pltpu.VMEM((tm + CONF_HALO, CHUNK), F32),
            pltpu.VMEM((tm + SC_HALO, CHUNK), F32),
            pltpu.VMEM((tm, CHUNK), BF16),
            pltpu.VMEM((tm, CHUNK), BF16),
        ],
        compiler_params=pltpu.CompilerParams(
            dimension_semantics=("arbitrary", "arbitrary"), vmem_limit_bytes=VMEM_LIMIT_BYTES),
        name="post",
    )(x, o, hb, hb, m, m, bg, g, cw, cb, lng, lnb, scw, wa, wb, wc, wo, fg, w1, w2)


def _pick_tile(n, want):
    t = min(n, want)
    while n % t:
        t //= 2
    return t


def _ffn_chunk(d_ff):
    for c in (512, 256, 128):
        if d_ff % c == 0:
            return c
    raise ValueError(f"d_ff={d_ff} must be a multiple of {LANES}")


def kernel(x, mix_norm_g, w_in, q_norm_g, k_norm_g, w_attn_out, conf_dw_w, conf_dw_b,
           conf_ln_g, conf_ln_b, w_conf_out, sc_conv_w, w_sc_out, gate_b, w_o,
           ffn_norm_g, w_ffn_in, w_ffn_out):
    b, s, d = x.shape
    depth = w_in.shape[0]
    d_ff = w_ffn_out.shape[1]
    assert w_in.shape[2] == 8 * CHUNK + 3 * d and N_HEADS * HEAD_DIM == CHUNK
    assert s % ATTN_BLOCK == 0
    tm_in = _pick_tile(b * s, 512)
    tm_post = _pick_tile(s, 512)
    assert tm_post % CONF_HALO == 0
    fc = _ffn_chunk(d_ff)
    act_dtype = BF16

    head_id = np.arange(CHUNK) // HEAD_DIM
    hmean = jnp.asarray((head_id[:, None] == head_id[None, :]) / HEAD_DIM, dtype=BF16)
    scale = float(1.0 / np.sqrt(HEAD_DIM))

    for l in range(depth):
        qg = (jnp.tile(q_norm_g[l], N_HEADS) * scale)[None, :]
        kg = jnp.tile(k_norm_g[l], N_HEADS)[None, :]
        q, k, v, hb, m, bg, g = _in_proj(
            x.reshape(b * s, d), mix_norm_g[l][None, :], w_in[l].astype(BF16), hmean, qg, kg,
            gate_b[l].reshape(1, -1), tm=tm_in, act_dtype=act_dtype)
        shp = lambda a: a.reshape(b, s, a.shape[-1])
        o = _attention(shp(q), shp(k), shp(v))
        w1 = w_ffn_in[l].astype(BF16)
        w1 = jnp.concatenate(
            [w1[:, :d_ff].reshape(d, d_ff // fc, fc), w1[:, d_ff:].reshape(d, d_ff // fc, fc)],
            axis=2).reshape(d, 2 * d_ff)
        x = _post(
            x, o, shp(hb), shp(m), shp(bg), shp(g),
            conf_dw_w[l], conf_dw_b[l][None, :], conf_ln_g[l][None, :], conf_ln_b[l][None, :],
            sc_conv_w[l], w_attn_out[l].astype(BF16), w_conf_out[l].astype(BF16),
            w_sc_out[l].astype(BF16), w_o[l].astype(BF16), ffn_norm_g[l][None, :],
            w1, w_ffn_out[l].astype(BF16), tm=tm_post, ffn_chunk=fc)
    return x
```

```python
import functools

import jax
import jax.numpy as jnp
import numpy as np
from jax import lax
from jax.experimental import pallas as pl
from jax.experimental.pallas import tpu as pltpu

F32 = jnp.float32
BF16 = jnp.bfloat16

EPS = 1e-6
N_HEADS = 8
HEAD_DIM = 64
CONF_WIDTH = 31
SC_WIDTH = 3
LANES = 128
SUBLANES = 8
CHUNK = 512
N_BRANCH_CHUNKS = 8
CONV_GROUPS = 4
ATTN_ROWS = 128
ATTN_WINDOW = 256
ATTN_Q_PAR = 4
UNDERFLOW_BOUND = 88.0
VMEM_LIMIT_BYTES = 56 * 1024 * 1024


def _compiler_params(n_grid_axes):
    return pltpu.CompilerParams(
        dimension_semantics=("arbitrary",) * n_grid_axes, vmem_limit_bytes=VMEM_LIMIT_BYTES)


def _const_spec(shape):
    nd = len(shape)
    return pl.BlockSpec(shape, lambda *_: (0,) * nd, pipeline_mode=pl.Buffered(1))


def _sigmoid(x):
    return 0.5 * jnp.tanh(0.5 * x) + 0.5


def _halo_groups(width):
    return -(-(width - 1) // SUBLANES)


def _causal_conv_rows(ext_ref, w8_ref, width, g_out, n_groups):
    halo = _halo_groups(width)
    sub = lax.broadcasted_iota(jnp.int32, (1, SUBLANES, CHUNK), 1)
    base = g_out + halo
    y = None
    for b in range(min(SUBLANES, width)):
        n_a = (width - 1 - b) // SUBLANES + 1
        lo = 0 if b else 1
        u = None
        for a in range(n_a):
            tap = w8_ref[width - 1 - (SUBLANES * a + b)]
            t = tap * ext_ref[pl.ds(base - 1 + lo - a, n_groups + 1 - lo)]
            u = t if u is None else u + t
        if b == 0:
            y = u
        else:
            r = pltpu.roll(u, b, axis=1)
            y = y + jnp.where(sub >= b, r[1:], r[:-1])
    return y


def _in_proj_kernel(x_ref, w_ref, hmean_ref, cw_ref, scw_ref,
                    ng_ref, qkvg_ref, gb_ref, cb_ref, lng_ref, lnb_ref,
                    qkv_out, hbn_out, hc_out, g_out,
                    u_scr, hbe_scr, me_scr, bg_scr, *, tiles_per_seq):
    tm = x_ref.shape[0]
    groups = tm // SUBLANES
    conf_halo = _halo_groups(CONF_WIDTH)
    sc_halo = _halo_groups(SC_WIDTH)
    n_gate_chunks = g_out.shape[0]
    conv_rows = CONV_GROUPS * SUBLANES
    n_conv = groups // CONV_GROUPS

    @pl.when(pl.program_id(0) % tiles_per_seq == 0)
    def _():
        hbe_scr[0:conf_halo] = jnp.zeros((conf_halo, SUBLANES, CHUNK), F32)
        me_scr[0:sc_halo] = jnp.zeros((sc_halo, SUBLANES, CHUNK), F32)

    x = x_ref[...]
    ms = jnp.mean(x * x, axis=-1, keepdims=True)
    u_scr[...] = (x * lax.rsqrt(ms + EPS) * ng_ref[...]).astype(BF16)

    def proj(c):
        return jnp.dot(u_scr[...], w_ref[:, c * CHUNK:(c + 1) * CHUNK],
                       preferred_element_type=F32)

    def grouped(a):
        return a.reshape(groups, SUBLANES, CHUNK)

    cf_val = proj(3)
    hbe_scr[conf_halo:] = grouped(cf_val * _sigmoid(proj(4)))

    def conf_step(c):
        rows = pl.ds(pl.multiple_of(c * conv_rows, conv_rows), conv_rows)
        acc = _causal_conv_rows(hbe_scr, cw_ref, CONF_WIDTH, c * CONV_GROUPS, CONV_GROUPS)
        acc = acc.reshape(conv_rows, CHUNK) + cb_ref[...]
        mu = jnp.mean(acc, axis=-1, keepdims=True)
        cen = acc - mu
        var = jnp.mean(cen * cen, axis=-1, keepdims=True)
        y = cen * lax.rsqrt(var + EPS) * lng_ref[...] + lnb_ref[...]
        hbn_out[rows, :] = (y * _sigmoid(y)).astype(BF16)

    def sc_step(c):
        rows = pl.ds(pl.multiple_of(c * conv_rows, conv_rows), conv_rows)
        sc = _causal_conv_rows(me_scr, scw_ref, SC_WIDTH, c * CONV_GROUPS, CONV_GROUPS)
        hc = bg_scr[pl.ds(c * CONV_GROUPS, CONV_GROUPS)] * sc
        hc_out[rows, :] = hc.reshape(conv_rows, CHUNK).astype(BF16)

    sc_x = proj(5)
    bg_scr[...] = grouped(proj(6))
    me_scr[sc_halo:] = grouped(proj(7) * sc_x)

    for j in range(n_gate_chunks):
        g_out[j] = (proj(N_BRANCH_CHUNKS + j) + gb_ref[j]).astype(BF16)
    for i in range(3):
        a = proj(i)
        if i < 2:
            msq = jnp.dot((a * a).astype(BF16), hmean_ref[...], preferred_element_type=F32)
            a = a * lax.rsqrt(msq + EPS)
        qkv_out[i] = (a * qkvg_ref[i]).astype(BF16)

    steps_per_iter = 2 if n_conv % 2 == 0 else 1

    def conv_iter(i, carry):
        for s in range(steps_per_iter):
            conf_step(i * steps_per_iter + s)
            sc_step(i * steps_per_iter + s)
        return carry

    lax.fori_loop(0, n_conv // steps_per_iter, conv_iter, 0)

    hbe_scr[0:conf_halo] = hbe_scr[groups:groups + conf_halo]
    me_scr[0:sc_halo] = me_scr[groups:groups + sc_halo]


def _in_proj(x2, ng, w, hmean, qkvg, gb, cw8, cb, lng, lnb, scw8, *, tm, tiles_per_seq):
    t, d = x2.shape
    n_chunks = w.shape[1] // CHUNK
    n_gate_chunks = n_chunks - N_BRANCH_CHUNKS
    groups = tm // SUBLANES
    row = lambda i: (i, 0)
    stacked = lambda i: (0, i, 0)
    outs = [
        jax.ShapeDtypeStruct((3, t, CHUNK), BF16),
        jax.ShapeDtypeStruct((t, CHUNK), BF16),
        jax.ShapeDtypeStruct((t, CHUNK), BF16),
        jax.ShapeDtypeStruct((n_gate_chunks, t, CHUNK), BF16),
    ]
    out_specs = [
        pl.BlockSpec((3, tm, CHUNK), stacked),
        pl.BlockSpec((tm, CHUNK), row),
        pl.BlockSpec((tm, CHUNK), row),
        pl.BlockSpec((n_gate_chunks, tm, CHUNK), stacked),
    ]
    kern = functools.partial(_in_proj_kernel, tiles_per_seq=tiles_per_seq)
    return pl.pallas_call(
        kern,
        out_shape=outs,
        grid=(t // tm,),
        in_specs=[
            pl.BlockSpec((tm, d), row),
            _const_spec((d, n_chunks * CHUNK)),
            _const_spec((CHUNK, CHUNK)),
            _const_spec((CONF_WIDTH, SUBLANES, CHUNK)),
            _const_spec((SC_WIDTH, SUBLANES, CHUNK)),
            _const_spec((1, d)),
            _const_spec((3, 1, CHUNK)),
            _const_spec((n_gate_chunks, 1, CHUNK)),
            _const_spec((1, CHUNK)),
            _const_spec((1, CHUNK)),
            _const_spec((1, CHUNK)),
        ],
        out_specs=out_specs,
        scratch_shapes=[
            pltpu.VMEM((tm, d), BF16),
            pltpu.VMEM((groups + _halo_groups(CONF_WIDTH), SUBLANES, CHUNK), F32),
            pltpu.VMEM((groups + _halo_groups(SC_WIDTH), SUBLANES, CHUNK), F32),
            pltpu.VMEM((groups, SUBLANES, CHUNK), F32),
        ],
        compiler_params=_compiler_params(1),
        name="in_proj",
    )(x2, w, hmean, cw8, scw8, ng, qkvg, gb, cb, lng, lnb)


def _attn_kernel(q_ref, k_ref, v_ref, o_ref, c_scr, acc_scr):
    s_len = q_ref.shape[0]
    qr, win = ATTN_ROWS, ATTN_WINDOW
    n_q = s_len // qr
    lane = lax.broadcasted_iota(jnp.int32, (1, LANES), 1)
    head0 = lane < HEAD_DIM
    rows = lax.broadcasted_iota(jnp.int32, (win, win), 0)
    cols = lax.broadcasted_iota(jnp.int32, (win, win), 1)
    tri = (rows >= cols).astype(BF16)
    q_row = lax.broadcasted_iota(jnp.int32, (2 * qr, win), 0) & (qr - 1)
    k_col = lax.broadcasted_iota(jnp.int32, (2 * qr, win), 1)


    def softplus(z):
        neg_abs = lax.bitcast_convert_type(
            lax.bitcast_convert_type(z, jnp.uint32) | jnp.uint32(0x80000000), F32)
        return jnp.maximum(z, 0.0) + jnp.log(1.0 + jnp.exp(neg_abs))

    def rev_cumsum(sp, t):
        return jnp.dot(sp.astype(BF16), t, preferred_element_type=F32)

    def stacked_q(q0):
        qb = q_ref[pl.ds(q0, qr), :]
        zero = jnp.zeros((), BF16)
        return jnp.concatenate([jnp.where(head0, qb, zero), jnp.where(head0, zero, qb)], axis=0)

    def merge_heads(pv):
        return jnp.where(head0, pv[:qr], pv[qr:])

    def near_diagonal(j):
        zs, masks, sps, k0s = [], [], [], []
        for t in range(ATTN_Q_PAR):
            q0 = pl.multiple_of((j * ATTN_Q_PAR + t) * qr, qr)
            k0 = pl.multiple_of(jnp.maximum(q0 + qr - win, 0), qr)
            z = lax.dot_general(stacked_q(q0), k_ref[pl.ds(k0, win), :],
                                (((1,), (1,)), ((), ())), preferred_element_type=F32)
            causal = k_col < q_row + (q0 - k0)
            zs.append(z)
            masks.append(causal)
            sps.append(jnp.where(causal, softplus(z), 0.0))
            k0s.append(k0)
        cs_all = rev_cumsum(jnp.concatenate(sps, axis=0), tri)
        cmins = []
        for t in range(ATTN_Q_PAR):
            cs = cs_all[t * 2 * qr:(t + 1) * 2 * qr]
            p = jnp.where(masks[t], jnp.exp(zs[t] - cs), 0.0).astype(BF16)
            c = cs[:, 0:1]
            c_scr[t] = c
            cmins.append(jnp.min(c))
            pv = jnp.dot(p, v_ref[pl.ds(k0s[t], win), :], preferred_element_type=F32)
            acc_scr[t] = merge_heads(pv)
        return cmins, k0s

    def q_group(j, carry):
        cmins, k0s = near_diagonal(j)
        tri_far = tri[:qr, :qr]
        for t in range(ATTN_Q_PAR):
            q0 = pl.multiple_of((j * ATTN_Q_PAR + t) * qr, qr)

            def cond(st):
                kb, cmin = st
                return jnp.logical_and(kb >= 0, cmin < UNDERFLOW_BOUND)

            def body(st, t=t, q0=q0):
                kb, _ = st
                kf = pl.multiple_of(kb * qr, qr)
                z = lax.dot_general(stacked_q(q0), k_ref[pl.ds(kf, qr), :],
                                    (((1,), (1,)), ((), ())), preferred_element_type=F32)
                cs = rev_cumsum(softplus(z), tri_far)
                c_prev = c_scr[t]
                p = jnp.exp(z - cs - c_prev).astype(BF16)
                c = c_prev + cs[:, 0:1]
                c_scr[t] = c
                acc_scr[t] += merge_heads(
                    jnp.dot(p, v_ref[pl.ds(kf, qr), :], preferred_element_type=F32))
                return kb - 1, jnp.min(c)

            lax.while_loop(cond, body, (k0s[t] // qr - 1, cmins[t]))
            o_ref[pl.ds(q0, qr), :] = acc_scr[t].astype(o_ref.dtype)
        return carry

    lax.fori_loop(0, n_q // ATTN_Q_PAR, q_group, 0)


def _attention(qkv):
    _, b, s, w = qkv.shape
    n_pairs = w // LANES
    spec = lambda which: pl.BlockSpec((None, None, s, LANES), lambda bi, hp: (which, bi, 0, hp))
    return pl.pallas_call(
        _attn_kernel,
        out_shape=jax.ShapeDtypeStruct((b, s, w), BF16),
        grid=(b, n_pairs),
        in_specs=[spec(0), spec(1), spec(2)],
        out_specs=pl.BlockSpec((None, s, LANES), lambda bi, hp: (bi, 0, hp)),
        scratch_shapes=[
            pltpu.VMEM((ATTN_Q_PAR, 2 * ATTN_ROWS, 1), F32),
            pltpu.VMEM((ATTN_Q_PAR, ATTN_ROWS, LANES), F32),
        ],
        compiler_params=_compiler_params(2),
        name="stickbreak_attn",
    )(qkv, qkv, qkv)


def _post_kernel(x_ref, o_ref, hbn_ref, hc_ref, g_ref,
                 wa_ref, wb_ref, wc_ref, wo_ref, fg_ref, w1_ref, w2_ref,
                 out_ref, *, ffn_chunk):
    d = x_ref.shape[1]
    per_branch = d // CHUNK
    y_a = jnp.dot(o_ref[...], wa_ref[...], preferred_element_type=F32)
    y_b = jnp.dot(hbn_ref[...], wb_ref[...], preferred_element_type=F32)
    y_c = jnp.dot(hc_ref[...], wc_ref[...], preferred_element_type=F32)
    parts = []
    for c in range(per_branch):
        sl = slice(c * CHUNK, (c + 1) * CHUNK)
        gate = lambda j: _sigmoid(g_ref[j].astype(F32))
        parts.append(gate(c) * y_a[:, sl]
                     + gate(per_branch + c) * y_b[:, sl]
                     + gate(2 * per_branch + c) * y_c[:, sl])
    merged = jnp.concatenate(parts, axis=1).astype(BF16)
    x1 = x_ref[...] + jnp.dot(merged, wo_ref[...], preferred_element_type=F32)

    ms = jnp.mean(x1 * x1, axis=-1, keepdims=True)
    xn = (x1 * lax.rsqrt(ms + EPS) * fg_ref[...]).astype(BF16)
    fc = ffn_chunk
    d_ff = w2_ref.shape[0]
    y = x1
    for c in range(d_ff // fc):
        gt = jnp.dot(xn, w1_ref[:, fc * c:fc * (c + 1)], preferred_element_type=F32)
        up = jnp.dot(xn, w1_ref[:, d_ff + fc * c:d_ff + fc * (c + 1)],
                     preferred_element_type=F32)
        act = (gt * _sigmoid(gt) * up).astype(BF16)
        y = y + jnp.dot(act, w2_ref[fc * c:fc * (c + 1), :], preferred_element_type=F32)
    out_ref[...] = y


def _post(x2, o, hbn, hc, g, wa, wb, wc, wo, fg, w1, w2, *, tm, ffn_chunk):
    t, d = x2.shape
    n_gate_chunks = g.shape[0]
    d_ff = w2.shape[0]
    tile = lambda width: pl.BlockSpec((tm, width), lambda i: (i, 0))
    kern = functools.partial(_post_kernel, ffn_chunk=ffn_chunk)
    return pl.pallas_call(
        kern,
        out_shape=jax.ShapeDtypeStruct((t, d), F32),
        grid=(t // tm,),
        in_specs=[
            tile(d), tile(CHUNK), tile(CHUNK), tile(CHUNK),
            pl.BlockSpec((n_gate_chunks, tm, CHUNK), lambda i: (0, i, 0)),
            _const_spec((CHUNK, d)), _const_spec((CHUNK, d)), _const_spec((CHUNK, d)),
            _const_spec((d, d)), _const_spec((1, d)),
            _const_spec((d, 2 * d_ff)), _const_spec((d_ff, d)),
        ],
        out_specs=tile(d),
        compiler_params=_compiler_params(1),
        name="post",
    )(x2, o, hbn, hc, g, wa, wb, wc, wo, fg, w1, w2)


def _pick_tile(n, want):
    t = min(n, want)
    while n % t:
        t //= 2
    return t


def _ffn_chunk(d_ff):
    for c in (512, 256, 128):
        if d_ff % c == 0:
            return c
    raise ValueError(f"d_ff={d_ff} must be a multiple of {LANES}")


def kernel(x, mix_norm_g, w_in, q_norm_g, k_norm_g, w_attn_out, conf_dw_w, conf_dw_b,
           conf_ln_g, conf_ln_b, w_conf_out, sc_conv_w, w_sc_out, gate_b, w_o,
           ffn_norm_g, w_ffn_in, w_ffn_out):
    b, s, d = x.shape
    depth = w_in.shape[0]
    d_ff = w_ffn_out.shape[1]
    n_chunks = w_in.shape[2] // CHUNK
    assert w_in.shape[2] == N_BRANCH_CHUNKS * CHUNK + 3 * d and N_HEADS * HEAD_DIM == CHUNK
    assert d % CHUNK == 0 and s % (ATTN_ROWS * ATTN_Q_PAR) == 0 and s >= ATTN_WINDOW
    tm_in = _pick_tile(s, 512)
    tm_post = _pick_tile(b * s, 512)
    assert tm_in % (CONV_GROUPS * SUBLANES) == 0 and tm_in // SUBLANES >= _halo_groups(CONF_WIDTH)
    fc = _ffn_chunk(d_ff)

    head_id = np.arange(CHUNK) // HEAD_DIM
    hmean = jnp.asarray((head_id[:, None] == head_id[None, :]) / HEAD_DIM, dtype=BF16)
    scale = float(1.0 / np.sqrt(HEAD_DIM))
    sub_bcast = lambda w: jnp.broadcast_to(w[:, None, :], (w.shape[0], SUBLANES, w.shape[1]))

    x2 = x.reshape(b * s, d)
    for l in range(depth):
        qkvg = jnp.stack([jnp.tile(q_norm_g[l], N_HEADS) * scale,
                          jnp.tile(k_norm_g[l], N_HEADS),
                          jnp.ones((CHUNK,), F32)])[:, None, :]
        qkv, hbn, hc, g = _in_proj(
            x2, mix_norm_g[l][None, :], w_in[l].astype(BF16), hmean, qkvg,
            gate_b[l].reshape(n_chunks - N_BRANCH_CHUNKS, 1, CHUNK),
            sub_bcast(conf_dw_w[l]), conf_dw_b[l][None, :],
            conf_ln_g[l][None, :], conf_ln_b[l][None, :], sub_bcast(sc_conv_w[l]),
            tm=tm_in, tiles_per_seq=s // tm_in)
        o = _attention(qkv.reshape(3, b, s, CHUNK)).reshape(b * s, CHUNK)
        x2 = _post(
            x2, o, hbn, hc, g, w_attn_out[l].astype(BF16), w_conf_out[l].astype(BF16),
            w_sc_out[l].astype(BF16), w_o[l].astype(BF16), ffn_norm_g[l][None, :],
            w_ffn_in[l].astype(BF16), w_ffn_out[l].astype(BF16), tm=tm_post, ffn_chunk=fc)
    return x2.reshape(b, s, d)
```

```python
import functools

import jax
import jax.numpy as jnp
import numpy as np
from jax import lax
from jax.experimental import pallas as pl
from jax.experimental.pallas import tpu as pltpu

F32 = jnp.float32
BF16 = jnp.bfloat16

EPS = 1e-6
N_HEADS = 8
HEAD_DIM = 64
CONF_WIDTH = 31
SC_WIDTH = 3
LANES = 128
SUBLANES = 8
CHUNK = 512
N_BRANCH_CHUNKS = 8
CONV_GROUPS = 4
ATTN_ROWS = 128
ATTN_WINDOW = 256
ATTN_Q_PAR = 8
UNDERFLOW_BOUND = 88.0
VMEM_LIMIT_BYTES = 56 * 1024 * 1024


def _compiler_params(n_grid_axes):
    return pltpu.CompilerParams(
        dimension_semantics=("arbitrary",) * n_grid_axes, vmem_limit_bytes=VMEM_LIMIT_BYTES)


def _const_spec(shape):
    nd = len(shape)
    return pl.BlockSpec(shape, lambda *_: (0,) * nd, pipeline_mode=pl.Buffered(1))


def _sigmoid(x):
    return 0.5 * jnp.tanh(0.5 * x) + 0.5


def _halo_groups(width):
    return -(-(width - 1) // SUBLANES)


def _causal_conv_rows(ext_ref, w8_ref, width, g_out, n_groups):
    halo = _halo_groups(width)
    sub = lax.broadcasted_iota(jnp.int32, (1, SUBLANES, CHUNK), 1)
    base = g_out + halo
    y = None
    for b in range(min(SUBLANES, width)):
        n_a = (width - 1 - b) // SUBLANES + 1
        lo = 0 if b else 1
        u = None
        for a in range(n_a):
            tap = w8_ref[width - 1 - (SUBLANES * a + b)]
            t = tap * ext_ref[pl.ds(base - 1 + lo - a, n_groups + 1 - lo)]
            u = t if u is None else u + t
        if b == 0:
            y = u
        else:
            r = pltpu.roll(u, b, axis=1)
            y = y + jnp.where(sub >= b, r[1:], r[:-1])
    return y


def _in_proj_kernel(x_ref, w_ref, hmean_ref, cw_ref, scw_ref,
                    ng_ref, qkvg_ref, gb_ref, cb_ref, lng_ref, lnb_ref,
                    qkv_out, hbn_out, hc_out, g_out,
                    u_scr, hbe_scr, me_scr, bg_scr, *, tiles_per_seq):
    tm = x_ref.shape[0]
    groups = tm // SUBLANES
    conf_halo = _halo_groups(CONF_WIDTH)
    sc_halo = _halo_groups(SC_WIDTH)
    n_gate_chunks = g_out.shape[0]
    conv_rows = CONV_GROUPS * SUBLANES
    n_conv = groups // CONV_GROUPS

    @pl.when(pl.program_id(0) % tiles_per_seq == 0)
    def _():
        hbe_scr[0:conf_halo] = jnp.zeros((conf_halo, SUBLANES, CHUNK), F32)
        me_scr[0:sc_halo] = jnp.zeros((sc_halo, SUBLANES, CHUNK), F32)

    x = x_ref[...]
    ms = jnp.mean(x * x, axis=-1, keepdims=True)
    u_scr[...] = (x * lax.rsqrt(ms + EPS) * ng_ref[...]).astype(BF16)

    def proj(c):
        return jnp.dot(u_scr[...], w_ref[:, c * CHUNK:(c + 1) * CHUNK],
                       preferred_element_type=F32)

    def grouped(a):
        return a.reshape(groups, SUBLANES, CHUNK)

    cf_val = proj(3)
    hbe_scr[conf_halo:] = grouped(cf_val * _sigmoid(proj(4)))

    def conf_step(c):
        rows = pl.ds(pl.multiple_of(c * conv_rows, conv_rows), conv_rows)
        acc = _causal_conv_rows(hbe_scr, cw_ref, CONF_WIDTH, c * CONV_GROUPS, CONV_GROUPS)
        acc = acc.reshape(conv_rows, CHUNK) + cb_ref[...]
        mu = jnp.mean(acc, axis=-1, keepdims=True)
        cen = acc - mu
        var = jnp.mean(cen * cen, axis=-1, keepdims=True)
        y = cen * lax.rsqrt(var + EPS) * lng_ref[...] + lnb_ref[...]
        hbn_out[rows, :] = (y * _sigmoid(y)).astype(BF16)

    def sc_step(c):
        rows = pl.ds(pl.multiple_of(c * conv_rows, conv_rows), conv_rows)
        sc = _causal_conv_rows(me_scr, scw_ref, SC_WIDTH, c * CONV_GROUPS, CONV_GROUPS)
        hc = bg_scr[pl.ds(c * CONV_GROUPS, CONV_GROUPS)] * sc
        hc_out[rows, :] = hc.reshape(conv_rows, CHUNK).astype(BF16)

    sc_x = proj(5)
    bg_scr[...] = grouped(proj(6))
    me_scr[sc_halo:] = grouped(proj(7) * sc_x)

    for j in range(n_gate_chunks):
        g_out[j] = (proj(N_BRANCH_CHUNKS + j) + gb_ref[j]).astype(BF16)
    for i in range(3):
        a = proj(i)
        if i < 2:
            msq = jnp.dot((a * a).astype(BF16), hmean_ref[...], preferred_element_type=F32)
            a = a * lax.rsqrt(msq + EPS)
        qkv_out[i] = (a * qkvg_ref[i]).astype(BF16)

    steps_per_iter = 2 if n_conv % 2 == 0 else 1

    def conv_iter(i, carry):
        for s in range(steps_per_iter):
            conf_step(i * steps_per_iter + s)
            sc_step(i * steps_per_iter + s)
        return carry

    lax.fori_loop(0, n_conv // steps_per_iter, conv_iter, 0)

    hbe_scr[0:conf_halo] = hbe_scr[groups:groups + conf_halo]
    me_scr[0:sc_halo] = me_scr[groups:groups + sc_halo]


def _in_proj(x2, ng, w, hmean, qkvg, gb, cw8, cb, lng, lnb, scw8, *, tm, tiles_per_seq):
    t, d = x2.shape
    n_chunks = w.shape[1] // CHUNK
    n_gate_chunks = n_chunks - N_BRANCH_CHUNKS
    groups = tm // SUBLANES
    row = lambda i: (i, 0)
    stacked = lambda i: (0, i, 0)
    outs = [
        jax.ShapeDtypeStruct((3, t, CHUNK), BF16),
        jax.ShapeDtypeStruct((t, CHUNK), BF16),
        jax.ShapeDtypeStruct((t, CHUNK), BF16),
        jax.ShapeDtypeStruct((n_gate_chunks, t, CHUNK), BF16),
    ]
    out_specs = [
        pl.BlockSpec((3, tm, CHUNK), stacked),
        pl.BlockSpec((tm, CHUNK), row),
        pl.BlockSpec((tm, CHUNK), row),
        pl.BlockSpec((n_gate_chunks, tm, CHUNK), stacked),
    ]
    kern = functools.partial(_in_proj_kernel, tiles_per_seq=tiles_per_seq)
    return pl.pallas_call(
        kern,
        out_shape=outs,
        grid=(t // tm,),
        in_specs=[
            pl.BlockSpec((tm, d), row),
            _const_spec((d, n_chunks * CHUNK)),
            _const_spec((CHUNK, CHUNK)),
            _const_spec((CONF_WIDTH, SUBLANES, CHUNK)),
            _const_spec((SC_WIDTH, SUBLANES, CHUNK)),
            _const_spec((1, d)),
            _const_spec((3, 1, CHUNK)),
            _const_spec((n_gate_chunks, 1, CHUNK)),
            _const_spec((1, CHUNK)),
            _const_spec((1, CHUNK)),
            _const_spec((1, CHUNK)),
        ],
        out_specs=out_specs,
        scratch_shapes=[
            pltpu.VMEM((tm, d), BF16),
            pltpu.VMEM((groups + _halo_groups(CONF_WIDTH), SUBLANES, CHUNK), F32),
            pltpu.VMEM((groups + _halo_groups(SC_WIDTH), SUBLANES, CHUNK), F32),
            pltpu.VMEM((groups, SUBLANES, CHUNK), F32),
        ],
        compiler_params=_compiler_params(1),
        name="in_proj",
    )(x2, w, hmean, cw8, scw8, ng, qkvg, gb, cb, lng, lnb)


def _attn_kernel(q_ref, k_ref, v_ref, o_ref, c_scr, acc_scr):
    s_len = q_ref.shape[0]
    qr, win = ATTN_ROWS, ATTN_WINDOW
    n_q = s_len // qr
    lane = lax.broadcasted_iota(jnp.int32, (1, LANES), 1)
    head0 = lane < HEAD_DIM
    rows = lax.broadcasted_iota(jnp.int32, (win, win), 0)
    cols = lax.broadcasted_iota(jnp.int32, (win, win), 1)
    tri = (rows >= cols).astype(BF16)
    q_row = lax.broadcasted_iota(jnp.int32, (2 * qr, win), 0) & (qr - 1)
    k_col = lax.broadcasted_iota(jnp.int32, (2 * qr, win), 1)


    def softplus(z):
        neg_abs = lax.bitcast_convert_type(
            lax.bitcast_convert_type(z, jnp.uint32) | jnp.uint32(0x80000000), F32)
        return jnp.maximum(z, 0.0) + jnp.log(1.0 + jnp.exp(neg_abs))

    def rev_cumsum(sp, t):
        return jnp.dot(sp.astype(BF16), t, preferred_element_type=F32)

    def stacked_q(q0):
        qb = q_ref[pl.ds(q0, qr), :]
        zero = jnp.zeros((), BF16)
        return jnp.concatenate([jnp.where(head0, qb, zero), jnp.where(head0, zero, qb)], axis=0)

    def merge_heads(pv):
        return jnp.where(head0, pv[:qr], pv[qr:])

    def near_diagonal(j):
        zs, masks, sps, k0s = [], [], [], []
        for t in range(ATTN_Q_PAR):
            q0 = pl.multiple_of((j * ATTN_Q_PAR + t) * qr, qr)
            k0 = pl.multiple_of(jnp.maximum(q0 + qr - win, 0), qr)
            z = lax.dot_general(stacked_q(q0), k_ref[pl.ds(k0, win), :],
                                (((1,), (1,)), ((), ())), preferred_element_type=F32)
            causal = k_col < q_row + (q0 - k0)
            zs.append(z)
            masks.append(causal)
            sps.append(jnp.where(causal, softplus(z), 0.0))
            k0s.append(k0)
        cs_all = rev_cumsum(jnp.concatenate(sps, axis=0), tri)
        cmins = []
        for t in range(ATTN_Q_PAR):
            cs = cs_all[t * 2 * qr:(t + 1) * 2 * qr]
            p = jnp.where(masks[t], jnp.exp(zs[t] - cs), 0.0).astype(BF16)
            c = cs[:, 0:1]
            c_scr[t] = c
            cmins.append(jnp.min(c))
            pv = jnp.dot(p, v_ref[pl.ds(k0s[t], win), :], preferred_element_type=F32)
            acc_scr[t] = merge_heads(pv)
        return cmins

    def q_group(j, carry):
        cmins = near_diagonal(j)
        first = j * ATTN_Q_PAR
        q0s = [pl.multiple_of((first + t) * qr, qr) for t in range(ATTN_Q_PAR)]
        tri_far = tri[:qr, :qr]

        def next_key_block(t, step):
            return first + t - 2 - step

        def cond(st):
            step, cm = st[0], st[1:]
            todo = [jnp.logical_and(next_key_block(t, step) >= 0, cm[t] < UNDERFLOW_BOUND)
                    for t in range(ATTN_Q_PAR)]
            return functools.reduce(jnp.logical_or, todo)

        def body(st):
            step = st[0]
            zs, sps, kfs, valids = [], [], [], []
            for t in range(ATTN_Q_PAR):
                kb = next_key_block(t, step)
                kf = pl.multiple_of(jnp.maximum(kb, 0) * qr, qr)
                z = lax.dot_general(stacked_q(q0s[t]), k_ref[pl.ds(kf, qr), :],
                                    (((1,), (1,)), ((), ())), preferred_element_type=F32)
                zs.append(z)
                sps.append(softplus(z))
                kfs.append(kf)
                valids.append(kb >= 0)
            cs_all = rev_cumsum(jnp.concatenate(sps, axis=0), tri_far)
            new_cmins = []
            for t in range(ATTN_Q_PAR):
                cs = cs_all[t * 2 * qr:(t + 1) * 2 * qr]
                c_prev = c_scr[t]
                p = jnp.where(valids[t], jnp.exp(zs[t] - cs - c_prev), 0.0).astype(BF16)
                c = c_prev + jnp.where(valids[t], cs[:, 0:1], 0.0)
                c_scr[t] = c
                acc_scr[t] += merge_heads(
                    jnp.dot(p, v_ref[pl.ds(kfs[t], qr), :], preferred_element_type=F32))
                new_cmins.append(jnp.min(c))
            return (step + 1, *new_cmins)

        lax.while_loop(cond, body, (jnp.int32(0), *cmins))
        for t in range(ATTN_Q_PAR):
            o_ref[pl.ds(q0s[t], qr), :] = acc_scr[t].astype(o_ref.dtype)
        return carry

    lax.fori_loop(0, n_q // ATTN_Q_PAR, q_group, 0)


def _attention(qkv):
    _, b, s, w = qkv.shape
    n_pairs = w // LANES
    spec = lambda which: pl.BlockSpec((None, None, s, LANES), lambda bi, hp: (which, bi, 0, hp))
    return pl.pallas_call(
        _attn_kernel,
        out_shape=jax.ShapeDtypeStruct((b, s, w), BF16),
        grid=(b, n_pairs),
        in_specs=[spec(0), spec(1), spec(2)],
        out_specs=pl.BlockSpec((None, s, LANES), lambda bi, hp: (bi, 0, hp)),
        scratch_shapes=[
            pltpu.VMEM((ATTN_Q_PAR, 2 * ATTN_ROWS, 1), F32),
            pltpu.VMEM((ATTN_Q_PAR, ATTN_ROWS, LANES), F32),
        ],
        compiler_params=_compiler_params(2),
        name="stickbreak_attn",
    )(qkv, qkv, qkv)


def _post_kernel(x_ref, o_ref, hbn_ref, hc_ref, g_ref,
                 wa_ref, wb_ref, wc_ref, wo_ref, fg_ref, w1_ref, w2_ref,
                 out_ref, *, ffn_chunk):
    d = x_ref.shape[1]
    per_branch = d // CHUNK
    y_a = jnp.dot(o_ref[...], wa_ref[...], preferred_element_type=F32)
    y_b = jnp.dot(hbn_ref[...], wb_ref[...], preferred_element_type=F32)
    y_c = jnp.dot(hc_ref[...], wc_ref[...], preferred_element_type=F32)
    parts = []
    for c in range(per_branch):
        sl = slice(c * CHUNK, (c + 1) * CHUNK)
        gate = lambda j: _sigmoid(g_ref[j].astype(F32))
        parts.append(gate(c) * y_a[:, sl]
                     + gate(per_branch + c) * y_b[:, sl]
                     + gate(2 * per_branch + c) * y_c[:, sl])
    merged = jnp.concatenate(parts, axis=1).astype(BF16)
    x1 = x_ref[...] + jnp.dot(merged, wo_ref[...], preferred_element_type=F32)

    ms = jnp.mean(x1 * x1, axis=-1, keepdims=True)
    xn = (x1 * lax.rsqrt(ms + EPS) * fg_ref[...]).astype(BF16)
    fc = ffn_chunk
    d_ff = w2_ref.shape[0]
    y = x1
    for c in range(d_ff // fc):
        gt = jnp.dot(xn, w1_ref[:, fc * c:fc * (c + 1)], preferred_element_type=F32)
        up = jnp.dot(xn, w1_ref[:, d_ff + fc * c:d_ff + fc * (c + 1)],
                     preferred_element_type=F32)
        act = (gt * _sigmoid(gt) * up).astype(BF16)
        y = y + jnp.dot(act, w2_ref[fc * c:fc * (c + 1), :], preferred_element_type=F32)
    out_ref[...] = y


def _post(x2, o, hbn, hc, g, wa, wb, wc, wo, fg, w1, w2, *, tm, ffn_chunk):
    t, d = x2.shape
    n_gate_chunks = g.shape[0]
    d_ff = w2.shape[0]
    tile = lambda width: pl.BlockSpec((tm, width), lambda i: (i, 0))
    kern = functools.partial(_post_kernel, ffn_chunk=ffn_chunk)
    return pl.pallas_call(
        kern,
        out_shape=jax.ShapeDtypeStruct((t, d), F32),
        grid=(t // tm,),
        in_specs=[
            tile(d), tile(CHUNK), tile(CHUNK), tile(CHUNK),
            pl.BlockSpec((n_gate_chunks, tm, CHUNK), lambda i: (0, i, 0)),
            _const_spec((CHUNK, d)), _const_spec((CHUNK, d)), _const_spec((CHUNK, d)),
            _const_spec((d, d)), _const_spec((1, d)),
            _const_spec((d, 2 * d_ff)), _const_spec((d_ff, d)),
        ],
        out_specs=tile(d),
        compiler_params=_compiler_params(1),
        name="post",
    )(x2, o, hbn, hc, g, wa, wb, wc, wo, fg, w1, w2)


def _pick_tile(n, want):
    t = min(n, want)
    while n % t:
        t //= 2
    return t


def _ffn_chunk(d_ff):
    for c in (512, 256, 128):
        if d_ff % c == 0:
            return c
    raise ValueError(f"d_ff={d_ff} must be a multiple of {LANES}")


def kernel(x, mix_norm_g, w_in, q_norm_g, k_norm_g, w_attn_out, conf_dw_w, conf_dw_b,
           conf_ln_g, conf_ln_b, w_conf_out, sc_conv_w, w_sc_out, gate_b, w_o,
           ffn_norm_g, w_ffn_in, w_ffn_out):
    b, s, d = x.shape
    depth = w_in.shape[0]
    d_ff = w_ffn_out.shape[1]
    n_chunks = w_in.shape[2] // CHUNK
    assert w_in.shape[2] == N_BRANCH_CHUNKS * CHUNK + 3 * d and N_HEADS * HEAD_DIM == CHUNK
    assert d % CHUNK == 0 and s % (ATTN_ROWS * ATTN_Q_PAR) == 0 and s >= ATTN_WINDOW
    tm_in = _pick_tile(s, 512)
    tm_post = _pick_tile(b * s, 512)
    assert tm_in % (CONV_GROUPS * SUBLANES) == 0 and tm_in // SUBLANES >= _halo_groups(CONF_WIDTH)
    fc = _ffn_chunk(d_ff)

    head_id = np.arange(CHUNK) // HEAD_DIM
    hmean = jnp.asarray((head_id[:, None] == head_id[None, :]) / HEAD_DIM, dtype=BF16)
    scale = float(1.0 / np.sqrt(HEAD_DIM))
    sub_bcast = lambda w: jnp.broadcast_to(w[:, None, :], (w.shape[0], SUBLANES, w.shape[1]))

    x2 = x.reshape(b * s, d)
    for l in range(depth):
        qkvg = jnp.stack([jnp.tile(q_norm_g[l], N_HEADS) * scale,
                          jnp.tile(k_norm_g[l], N_HEADS),
                          jnp.ones((CHUNK,), F32)])[:, None, :]
        qkv, hbn, hc, g = _in_proj(
            x2, mix_norm_g[l][None, :], w_in[l].astype(BF16), hmean, qkvg,
            gate_b[l].reshape(n_chunks - N_BRANCH_CHUNKS, 1, CHUNK),
            sub_bcast(conf_dw_w[l]), conf_dw_b[l][None, :],
            conf_ln_g[l][None, :], conf_ln_b[l][None, :], sub_bcast(sc_conv_w[l]),
            tm=tm_in, tiles_per_seq=s // tm_in)
        o = _attention(qkv.reshape(3, b, s, CHUNK)).reshape(b * s, CHUNK)
        x2 = _post(
            x2, o, hbn, hc, g, w_attn_out[l].astype(BF16), w_conf_out[l].astype(BF16),
            w_sc_out[l].astype(BF16), w_o[l].astype(BF16), ffn_norm_g[l][None, :],
            w_ffn_in[l].astype(BF16), w_ffn_out[l].astype(BF16), tm=tm_post, ffn_chunk=fc)
    return x2.reshape(b, s, d)
```

```python
import functools

import jax
import jax.numpy as jnp
import numpy as np
from jax import lax
from jax.experimental import pallas as pl
from jax.experimental.pallas import tpu as pltpu

F32 = jnp.float32
BF16 = jnp.bfloat16

EPS = 1e-6
N_HEADS = 8
HEAD_DIM = 64
CONF_WIDTH = 31
SC_WIDTH = 3
LANES = 128
SUBLANES = 8
CHUNK = 512
N_BRANCH_CHUNKS = 8
CONV_GROUPS = 8
ATTN_ROWS = 128
ATTN_WINDOW = 256
ATTN_Q_PAR = 8
UNDERFLOW_BOUND = 88.0
VMEM_LIMIT_BYTES = 56 * 1024 * 1024


def _compiler_params(n_grid_axes):
    return pltpu.CompilerParams(
        dimension_semantics=("arbitrary",) * n_grid_axes, vmem_limit_bytes=VMEM_LIMIT_BYTES)


def _const_spec(shape):
    nd = len(shape)
    return pl.BlockSpec(shape, lambda *_: (0,) * nd, pipeline_mode=pl.Buffered(1))


def _sigmoid(x):
    return 0.5 * jnp.tanh(0.5 * x) + 0.5


def _halo_groups(width):
    return -(-(width - 1) // SUBLANES)


def _causal_conv_rows(ext_ref, w8_ref, width, g_out, n_groups):
    halo = _halo_groups(width)
    sub = lax.broadcasted_iota(jnp.int32, (1, SUBLANES, CHUNK), 1)
    base = g_out + halo
    y = None
    for b in range(min(SUBLANES, width)):
        n_a = (width - 1 - b) // SUBLANES + 1
        lo = 0 if b else 1
        u = None
        for a in range(n_a):
            tap = w8_ref[width - 1 - (SUBLANES * a + b)]
            t = tap * ext_ref[pl.ds(base - 1 + lo - a, n_groups + 1 - lo)]
            u = t if u is None else u + t
        if b == 0:
            y = u
        else:
            r = pltpu.roll(u, b, axis=1)
            y = y + jnp.where(sub >= b, r[1:], r[:-1])
    return y


def _in_proj_kernel(x_ref, w_ref, hmean_ref, cw_ref, scw_ref,
                    ng_ref, qkvg_ref, gb_ref, cb_ref, lng_ref, lnb_ref,
                    qkv_out, hbn_out, hc_out, g_out,
                    u_scr, hbe_scr, me_scr, bg_scr, *, tiles_per_seq):
    tm = x_ref.shape[0]
    groups = tm // SUBLANES
    conf_halo = _halo_groups(CONF_WIDTH)
    sc_halo = _halo_groups(SC_WIDTH)
    n_gate_chunks = g_out.shape[0]
    conv_rows = CONV_GROUPS * SUBLANES
    n_conv = groups // CONV_GROUPS

    @pl.when(pl.program_id(0) % tiles_per_seq == 0)
    def _():
        hbe_scr[0:conf_halo] = jnp.zeros((conf_halo, SUBLANES, CHUNK), F32)
        me_scr[0:sc_halo] = jnp.zeros((sc_halo, SUBLANES, CHUNK), F32)

    x = x_ref[...]
    ms = jnp.mean(x * x, axis=-1, keepdims=True)
    u_scr[...] = (x * lax.rsqrt(ms + EPS) * ng_ref[...]).astype(BF16)

    def proj(c):
        return jnp.dot(u_scr[...], w_ref[:, c * CHUNK:(c + 1) * CHUNK],
                       preferred_element_type=F32)

    def grouped(a):
        return a.reshape(groups, SUBLANES, CHUNK)

    cf_val = proj(3)
    hbe_scr[conf_halo:] = grouped(cf_val * _sigmoid(proj(4)))

    def conf_step(c):
        rows = pl.ds(pl.multiple_of(c * conv_rows, conv_rows), conv_rows)
        acc = _causal_conv_rows(hbe_scr, cw_ref, CONF_WIDTH, c * CONV_GROUPS, CONV_GROUPS)
        acc = acc.reshape(conv_rows, CHUNK) + cb_ref[...]
        mu = jnp.mean(acc, axis=-1, keepdims=True)
        cen = acc - mu
        var = jnp.mean(cen * cen, axis=-1, keepdims=True)
        y = cen * lax.rsqrt(var + EPS) * lng_ref[...] + lnb_ref[...]
        hbn_out[rows, :] = (y * _sigmoid(y)).astype(BF16)

    def sc_step(c):
        rows = pl.ds(pl.multiple_of(c * conv_rows, conv_rows), conv_rows)
        sc = _causal_conv_rows(me_scr, scw_ref, SC_WIDTH, c * CONV_GROUPS, CONV_GROUPS)
        hc = bg_scr[pl.ds(c * CONV_GROUPS, CONV_GROUPS)] * sc
        hc_out[rows, :] = hc.reshape(conv_rows, CHUNK).astype(BF16)

    sc_x = proj(5)
    bg_scr[...] = grouped(proj(6))
    me_scr[sc_halo:] = grouped(proj(7) * sc_x)

    for j in range(n_gate_chunks):
        g_out[j] = (proj(N_BRANCH_CHUNKS + j) + gb_ref[j]).astype(BF16)
    for i in range(3):
        a = proj(i)
        if i < 2:
            msq = jnp.dot((a * a).astype(BF16), hmean_ref[...], preferred_element_type=F32)
            a = a * lax.rsqrt(msq + EPS)
        qkv_out[i] = (a * qkvg_ref[i]).astype(BF16)

    steps_per_iter = 2 if n_conv % 2 == 0 else 1

    def conv_iter(i, carry):
        for s in range(steps_per_iter):
            conf_step(i * steps_per_iter + s)
            sc_step(i * steps_per_iter + s)
        return carry

    lax.fori_loop(0, n_conv // steps_per_iter, conv_iter, 0)

    hbe_scr[0:conf_halo] = hbe_scr[groups:groups + conf_halo]
    me_scr[0:sc_halo] = me_scr[groups:groups + sc_halo]


def _in_proj(x2, ng, w, hmean, qkvg, gb, cw8, cb, lng, lnb, scw8, *, tm, tiles_per_seq):
    t, d = x2.shape
    n_chunks = w.shape[1] // CHUNK
    n_gate_chunks = n_chunks - N_BRANCH_CHUNKS
    groups = tm // SUBLANES
    row = lambda i: (i, 0)
    stacked = lambda i: (0, i, 0)
    outs = [
        jax.ShapeDtypeStruct((3, t, CHUNK), BF16),
        jax.ShapeDtypeStruct((t, CHUNK), BF16),
        jax.ShapeDtypeStruct((t, CHUNK), BF16),
        jax.ShapeDtypeStruct((n_gate_chunks, t, CHUNK), BF16),
    ]
    out_specs = [
        pl.BlockSpec((3, tm, CHUNK), stacked),
        pl.BlockSpec((tm, CHUNK), row),
        pl.BlockSpec((tm, CHUNK), row),
        pl.BlockSpec((n_gate_chunks, tm, CHUNK), stacked),
    ]
    kern = functools.partial(_in_proj_kernel, tiles_per_seq=tiles_per_seq)
    return pl.pallas_call(
        kern,
        out_shape=outs,
        grid=(t // tm,),
        in_specs=[
            pl.BlockSpec((tm, d), row),
            _const_spec((d, n_chunks * CHUNK)),
            _const_spec((CHUNK, CHUNK)),
            _const_spec((CONF_WIDTH, SUBLANES, CHUNK)),
            _const_spec((SC_WIDTH, SUBLANES, CHUNK)),
            _const_spec((1, d)),
            _const_spec((3, 1, CHUNK)),
            _const_spec((n_gate_chunks, 1, CHUNK)),
            _const_spec((1, CHUNK)),
            _const_spec((1, CHUNK)),
            _const_spec((1, CHUNK)),
        ],
        out_specs=out_specs,
        scratch_shapes=[
            pltpu.VMEM((tm, d), BF16),
            pltpu.VMEM((groups + _halo_groups(CONF_WIDTH), SUBLANES, CHUNK), F32),
            pltpu.VMEM((groups + _halo_groups(SC_WIDTH), SUBLANES, CHUNK), F32),
            pltpu.VMEM((groups, SUBLANES, CHUNK), F32),
        ],
        compiler_params=_compiler_params(1),
        name="in_proj",
    )(x2, w, hmean, cw8, scw8, ng, qkvg, gb, cb, lng, lnb)


def _attn_kernel(q_ref, k_ref, v_ref, o_ref, c_scr, acc_scr):
    s_len = q_ref.shape[0]
    qr, win = ATTN_ROWS, ATTN_WINDOW
    n_q = s_len // qr
    lane = lax.broadcasted_iota(jnp.int32, (1, LANES), 1)
    head0 = lane < HEAD_DIM
    rows = lax.broadcasted_iota(jnp.int32, (win, win), 0)
    cols = lax.broadcasted_iota(jnp.int32, (win, win), 1)
    tri = (rows >= cols).astype(BF16)
    q_row = lax.broadcasted_iota(jnp.int32, (2 * qr, win), 0) & (qr - 1)
    k_col = lax.broadcasted_iota(jnp.int32, (2 * qr, win), 1)


    def softplus(z):
        neg_abs = lax.bitcast_convert_type(
            lax.bitcast_convert_type(z, jnp.uint32) | jnp.uint32(0x80000000), F32)
        return jnp.maximum(z, 0.0) + jnp.log(1.0 + jnp.exp(neg_abs))

    def rev_cumsum(sp, t):
        return jnp.dot(sp.astype(BF16), t, preferred_element_type=F32)

    def stacked_q(q0):
        qb = q_ref[pl.ds(q0, qr), :]
        zero = jnp.zeros((), BF16)
        return jnp.concatenate([jnp.where(head0, qb, zero), jnp.where(head0, zero, qb)], axis=0)

    def merge_heads(pv):
        return jnp.where(head0, pv[:qr], pv[qr:])

    def near_diagonal(j):
        zs, masks, sps, k0s = [], [], [], []
        for t in range(ATTN_Q_PAR):
            q0 = pl.multiple_of((j * ATTN_Q_PAR + t) * qr, qr)
            k0 = pl.multiple_of(jnp.maximum(q0 + qr - win, 0), qr)
            z = lax.dot_general(stacked_q(q0), k_ref[pl.ds(k0, win), :],
                                (((1,), (1,)), ((), ())), preferred_element_type=F32)
            causal = k_col < q_row + (q0 - k0)
            zs.append(z)
            masks.append(causal)
            sps.append(jnp.where(causal, softplus(z), 0.0))
            k0s.append(k0)
        cs_all = rev_cumsum(jnp.concatenate(sps, axis=0), tri)
        cmins = []
        for t in range(ATTN_Q_PAR):
            cs = cs_all[t * 2 * qr:(t + 1) * 2 * qr]
            p = jnp.where(masks[t], jnp.exp(zs[t] - cs), 0.0).astype(BF16)
            c = cs[:, 0:1]
            c_scr[t] = c
            cmins.append(jnp.min(c))
            pv = jnp.dot(p, v_ref[pl.ds(k0s[t], win), :], preferred_element_type=F32)
            acc_scr[t] = merge_heads(pv)
        return cmins

    def q_group(j, carry):
        cmins = near_diagonal(j)
        first = j * ATTN_Q_PAR
        q0s = [pl.multiple_of((first + t) * qr, qr) for t in range(ATTN_Q_PAR)]
        tri_far = tri[:qr, :qr]

        def next_key_block(t, step):
            return first + t - 2 - step

        def cond(st):
            step, cm = st[0], st[1:]
            todo = [jnp.logical_and(next_key_block(t, step) >= 0, cm[t] < UNDERFLOW_BOUND)
                    for t in range(ATTN_Q_PAR)]
            return functools.reduce(jnp.logical_or, todo)

        def body(st):
            step = st[0]
            zs, sps, kfs, valids = [], [], [], []
            for t in range(ATTN_Q_PAR):
                kb = next_key_block(t, step)
                kf = pl.multiple_of(jnp.maximum(kb, 0) * qr, qr)
                z = lax.dot_general(stacked_q(q0s[t]), k_ref[pl.ds(kf, qr), :],
                                    (((1,), (1,)), ((), ())), preferred_element_type=F32)
                zs.append(z)
                sps.append(softplus(z))
                kfs.append(kf)
                valids.append(kb >= 0)
            cs_all = rev_cumsum(jnp.concatenate(sps, axis=0), tri_far)
            new_cmins = []
            for t in range(ATTN_Q_PAR):
                cs = cs_all[t * 2 * qr:(t + 1) * 2 * qr]
                c_prev = c_scr[t]
                p = jnp.where(valids[t], jnp.exp(zs[t] - cs - c_prev), 0.0).astype(BF16)
                c = c_prev + jnp.where(valids[t], cs[:, 0:1], 0.0)
                c_scr[t] = c
                acc_scr[t] += merge_heads(
                    jnp.dot(p, v_ref[pl.ds(kfs[t], qr), :], preferred_element_type=F32))
                new_cmins.append(jnp.min(c))
            return (step + 1, *new_cmins)

        lax.while_loop(cond, body, (jnp.int32(0), *cmins))
        for t in range(ATTN_Q_PAR):
            o_ref[pl.ds(q0s[t], qr), :] = acc_scr[t].astype(o_ref.dtype)
        return carry

    lax.fori_loop(0, n_q // ATTN_Q_PAR, q_group, 0)


def _attention(qkv):
    _, b, s, w = qkv.shape
    n_pairs = w // LANES
    spec = lambda which: pl.BlockSpec((None, None, s, LANES), lambda bi, hp: (which, bi, 0, hp))
    return pl.pallas_call(
        _attn_kernel,
        out_shape=jax.ShapeDtypeStruct((b, s, w), BF16),
        grid=(b, n_pairs),
        in_specs=[spec(0), spec(1), spec(2)],
        out_specs=pl.BlockSpec((None, s, LANES), lambda bi, hp: (bi, 0, hp)),
        scratch_shapes=[
            pltpu.VMEM((ATTN_Q_PAR, 2 * ATTN_ROWS, 1), F32),
            pltpu.VMEM((ATTN_Q_PAR, ATTN_ROWS, LANES), F32),
        ],
        compiler_params=_compiler_params(2),
        name="stickbreak_attn",
    )(qkv, qkv, qkv)


def _post_kernel(x_ref, o_ref, hbn_ref, hc_ref, g_ref,
                 wa_ref, wb_ref, wc_ref, wo_ref, fg_ref, w1_ref, w2_ref,
                 out_ref, *, ffn_chunk):
    d = x_ref.shape[1]
    per_branch = d // CHUNK
    y_a = jnp.dot(o_ref[...], wa_ref[...], preferred_element_type=F32)
    y_b = jnp.dot(hbn_ref[...], wb_ref[...], preferred_element_type=F32)
    y_c = jnp.dot(hc_ref[...], wc_ref[...], preferred_element_type=F32)
    parts = []
    for c in range(per_branch):
        sl = slice(c * CHUNK, (c + 1) * CHUNK)
        gate = lambda j: _sigmoid(g_ref[j].astype(F32))
        parts.append(gate(c) * y_a[:, sl]
                     + gate(per_branch + c) * y_b[:, sl]
                     + gate(2 * per_branch + c) * y_c[:, sl])
    merged = jnp.concatenate(parts, axis=1).astype(BF16)
    x1 = x_ref[...] + jnp.dot(merged, wo_ref[...], preferred_element_type=F32)

    ms = jnp.mean(x1 * x1, axis=-1, keepdims=True)
    xn = (x1 * lax.rsqrt(ms + EPS) * fg_ref[...]).astype(BF16)
    fc = ffn_chunk
    d_ff = w2_ref.shape[0]
    y = x1
    for c in range(d_ff // fc):
        gt = jnp.dot(xn, w1_ref[:, fc * c:fc * (c + 1)], preferred_element_type=F32)
        up = jnp.dot(xn, w1_ref[:, d_ff + fc * c:d_ff + fc * (c + 1)],
                     preferred_element_type=F32)
        act = (gt * _sigmoid(gt) * up).astype(BF16)
        y = y + jnp.dot(act, w2_ref[fc * c:fc * (c + 1), :], preferred_element_type=F32)
    out_ref[...] = y


def _post(x2, o, hbn, hc, g, wa, wb, wc, wo, fg, w1, w2, *, tm, ffn_chunk):
    t, d = x2.shape
    n_gate_chunks = g.shape[0]
    d_ff = w2.shape[0]
    tile = lambda width: pl.BlockSpec((tm, width), lambda i: (i, 0))
    kern = functools.partial(_post_kernel, ffn_chunk=ffn_chunk)
    return pl.pallas_call(
        kern,
        out_shape=jax.ShapeDtypeStruct((t, d), F32),
        grid=(t // tm,),
        in_specs=[
            tile(d), tile(CHUNK), tile(CHUNK), tile(CHUNK),
            pl.BlockSpec((n_gate_chunks, tm, CHUNK), lambda i: (0, i, 0)),
            _const_spec((CHUNK, d)), _const_spec((CHUNK, d)), _const_spec((CHUNK, d)),
            _const_spec((d, d)), _const_spec((1, d)),
            _const_spec((d, 2 * d_ff)), _const_spec((d_ff, d)),
        ],
        out_specs=tile(d),
        compiler_params=_compiler_params(1),
        name="post",
    )(x2, o, hbn, hc, g, wa, wb, wc, wo, fg, w1, w2)


def _pick_tile(n, want):
    t = min(n, want)
    while n % t:
        t //= 2
    return t


def _ffn_chunk(d_ff):
    for c in (512, 256, 128):
        if d_ff % c == 0:
            return c
    raise ValueError(f"d_ff={d_ff} must be a multiple of {LANES}")


def kernel(x, mix_norm_g, w_in, q_norm_g, k_norm_g, w_attn_out, conf_dw_w, conf_dw_b,
           conf_ln_g, conf_ln_b, w_conf_out, sc_conv_w, w_sc_out, gate_b, w_o,
           ffn_norm_g, w_ffn_in, w_ffn_out):
    b, s, d = x.shape
    depth = w_in.shape[0]
    d_ff = w_ffn_out.shape[1]
    n_chunks = w_in.shape[2] // CHUNK
    assert w_in.shape[2] == N_BRANCH_CHUNKS * CHUNK + 3 * d and N_HEADS * HEAD_DIM == CHUNK
    assert d % CHUNK == 0 and s % (ATTN_ROWS * ATTN_Q_PAR) == 0 and s >= ATTN_WINDOW
    tm_in = _pick_tile(s, 512)
    tm_post = _pick_tile(b * s, 512)
    assert tm_in % (CONV_GROUPS * SUBLANES) == 0 and tm_in // SUBLANES >= _halo_groups(CONF_WIDTH)
    fc = _ffn_chunk(d_ff)

    head_id = np.arange(CHUNK) // HEAD_DIM
    hmean = jnp.asarray((head_id[:, None] == head_id[None, :]) / HEAD_DIM, dtype=BF16)
    scale = float(1.0 / np.sqrt(HEAD_DIM))
    sub_bcast = lambda w: jnp.broadcast_to(w[:, None, :], (w.shape[0], SUBLANES, w.shape[1]))

    x2 = x.reshape(b * s, d)
    for l in range(depth):
        qkvg = jnp.stack([jnp.tile(q_norm_g[l], N_HEADS) * scale,
                          jnp.tile(k_norm_g[l], N_HEADS),
                          jnp.ones((CHUNK,), F32)])[:, None, :]
        qkv, hbn, hc, g = _in_proj(
            x2, mix_norm_g[l][None, :], w_in[l].astype(BF16), hmean, qkvg,
            gate_b[l].reshape(n_chunks - N_BRANCH_CHUNKS, 1, CHUNK),
            sub_bcast(conf_dw_w[l]), conf_dw_b[l][None, :],
            conf_ln_g[l][None, :], conf_ln_b[l][None, :], sub_bcast(sc_conv_w[l]),
            tm=tm_in, tiles_per_seq=s // tm_in)
        o = _attention(qkv.reshape(3, b, s, CHUNK)).reshape(b * s, CHUNK)
        x2 = _post(
            x2, o, hbn, hc, g, w_attn_out[l].astype(BF16), w_conf_out[l].astype(BF16),
            w_sc_out[l].astype(BF16), w_o[l].astype(BF16), ffn_norm_g[l][None, :],
            w_ffn_in[l].astype(BF16), w_ffn_out[l].astype(BF16), tm=tm_post, ffn_chunk=fc)
    return x2.reshape(b, s, d)
```

```python
import functools

import jax
import jax.numpy as jnp
import numpy as np
from jax import lax
from jax.experimental import pallas as pl
from jax.experimental.pallas import tpu as pltpu

F32 = jnp.float32
BF16 = jnp.bfloat16

EPS = 1e-6
N_HEADS = 8
HEAD_DIM = 64
CONF_WIDTH = 31
SC_WIDTH = 3
LANES = 128
SUBLANES = 8
CHUNK = 512
N_BRANCH_CHUNKS = 8
CONV_GROUPS = 8
ATTN_ROWS = 128
ATTN_WINDOW = 256
ATTN_Q_PAR = 8
ATTN_FAR_ROWS = 64
UNDERFLOW_BOUND = 88.0
VMEM_LIMIT_BYTES = 56 * 1024 * 1024


def _compiler_params(n_grid_axes):
    return pltpu.CompilerParams(
        dimension_semantics=("arbitrary",) * n_grid_axes, vmem_limit_bytes=VMEM_LIMIT_BYTES)


def _const_spec(shape):
    nd = len(shape)
    return pl.BlockSpec(shape, lambda *_: (0,) * nd, pipeline_mode=pl.Buffered(1))


def _sigmoid(x):
    return 0.5 * jnp.tanh(0.5 * x) + 0.5


def _halo_groups(width):
    return -(-(width - 1) // SUBLANES)


def _causal_conv_rows(ext_ref, w8_ref, width, g_out, n_groups):
    halo = _halo_groups(width)
    sub = lax.broadcasted_iota(jnp.int32, (1, SUBLANES, CHUNK), 1)
    base = g_out + halo
    y = None
    for b in range(min(SUBLANES, width)):
        n_a = (width - 1 - b) // SUBLANES + 1
        lo = 0 if b else 1
        u = None
        for a in range(n_a):
            tap = w8_ref[width - 1 - (SUBLANES * a + b)]
            t = tap * ext_ref[pl.ds(base - 1 + lo - a, n_groups + 1 - lo)]
            u = t if u is None else u + t
        if b == 0:
            y = u
        else:
            r = pltpu.roll(u, b, axis=1)
            y = y + jnp.where(sub >= b, r[1:], r[:-1])
    return y


def _in_proj_kernel(x_ref, w_ref, hmean_ref, cw_ref, scw_ref,
                    ng_ref, qkvg_ref, gb_ref, cb_ref, lng_ref, lnb_ref,
                    qkv_out, hbn_out, hc_out, g_out,
                    u_scr, hbe_scr, me_scr, bg_scr, *, tiles_per_seq):
    tm = x_ref.shape[0]
    groups = tm // SUBLANES
    conf_halo = _halo_groups(CONF_WIDTH)
    sc_halo = _halo_groups(SC_WIDTH)
    n_gate_chunks = g_out.shape[0]
    conv_rows = CONV_GROUPS * SUBLANES
    n_conv = groups // CONV_GROUPS

    @pl.when(pl.program_id(0) % tiles_per_seq == 0)
    def _():
        hbe_scr[0:conf_halo] = jnp.zeros((conf_halo, SUBLANES, CHUNK), F32)
        me_scr[0:sc_halo] = jnp.zeros((sc_halo, SUBLANES, CHUNK), F32)

    x = x_ref[...]
    ms = jnp.mean(x * x, axis=-1, keepdims=True)
    u_scr[...] = (x * lax.rsqrt(ms + EPS) * ng_ref[...]).astype(BF16)

    def proj(c):
        return jnp.dot(u_scr[...], w_ref[:, c * CHUNK:(c + 1) * CHUNK],
                       preferred_element_type=F32)

    def grouped(a):
        return a.reshape(groups, SUBLANES, CHUNK)

    cf_val = proj(3)
    hbe_scr[conf_halo:] = grouped(cf_val * _sigmoid(proj(4)))

    def conf_step(c):
        rows = pl.ds(pl.multiple_of(c * conv_rows, conv_rows), conv_rows)
        acc = _causal_conv_rows(hbe_scr, cw_ref, CONF_WIDTH, c * CONV_GROUPS, CONV_GROUPS)
        acc = acc.reshape(conv_rows, CHUNK) + cb_ref[...]
        mu = jnp.mean(acc, axis=-1, keepdims=True)
        cen = acc - mu
        var = jnp.mean(cen * cen, axis=-1, keepdims=True)
        y = cen * lax.rsqrt(var + EPS) * lng_ref[...] + lnb_ref[...]
        hbn_out[rows, :] = (y * _sigmoid(y)).astype(BF16)

    def sc_step(c):
        rows = pl.ds(pl.multiple_of(c * conv_rows, conv_rows), conv_rows)
        sc = _causal_conv_rows(me_scr, scw_ref, SC_WIDTH, c * CONV_GROUPS, CONV_GROUPS)
        hc = bg_scr[pl.ds(c * CONV_GROUPS, CONV_GROUPS)] * sc
        hc_out[rows, :] = hc.reshape(conv_rows, CHUNK).astype(BF16)

    sc_x = proj(5)
    bg_scr[...] = grouped(proj(6))
    me_scr[sc_halo:] = grouped(proj(7) * sc_x)

    for j in range(n_gate_chunks):
        g_out[j] = (proj(N_BRANCH_CHUNKS + j) + gb_ref[j]).astype(BF16)
    for i in range(3):
        a = proj(i)
        if i < 2:
            msq = jnp.dot((a * a).astype(BF16), hmean_ref[...], preferred_element_type=F32)
            a = a * lax.rsqrt(msq + EPS)
        qkv_out[i] = (a * qkvg_ref[i]).astype(BF16)

    steps_per_iter = 2 if n_conv % 2 == 0 else 1

    def conv_iter(i, carry):
        for s in range(steps_per_iter):
            conf_step(i * steps_per_iter + s)
            sc_step(i * steps_per_iter + s)
        return carry

    lax.fori_loop(0, n_conv // steps_per_iter, conv_iter, 0)

    hbe_scr[0:conf_halo] = hbe_scr[groups:groups + conf_halo]
    me_scr[0:sc_halo] = me_scr[groups:groups + sc_halo]


def _in_proj(x2, ng, w, hmean, qkvg, gb, cw8, cb, lng, lnb, scw8, *, tm, tiles_per_seq):
    t, d = x2.shape
    n_chunks = w.shape[1] // CHUNK
    n_gate_chunks = n_chunks - N_BRANCH_CHUNKS
    groups = tm // SUBLANES
    row = lambda i: (i, 0)
    stacked = lambda i: (0, i, 0)
    outs = [
        jax.ShapeDtypeStruct((3, t, CHUNK), BF16),
        jax.ShapeDtypeStruct((t, CHUNK), BF16),
        jax.ShapeDtypeStruct((t, CHUNK), BF16),
        jax.ShapeDtypeStruct((n_gate_chunks, t, CHUNK), BF16),
    ]
    out_specs = [
        pl.BlockSpec((3, tm, CHUNK), stacked),
        pl.BlockSpec((tm, CHUNK), row),
        pl.BlockSpec((tm, CHUNK), row),
        pl.BlockSpec((n_gate_chunks, tm, CHUNK), stacked),
    ]
    kern = functools.partial(_in_proj_kernel, tiles_per_seq=tiles_per_seq)
    return pl.pallas_call(
        kern,
        out_shape=outs,
        grid=(t // tm,),
        in_specs=[
            pl.BlockSpec((tm, d), row),
            _const_spec((d, n_chunks * CHUNK)),
            _const_spec((CHUNK, CHUNK)),
            _const_spec((CONF_WIDTH, SUBLANES, CHUNK)),
            _const_spec((SC_WIDTH, SUBLANES, CHUNK)),
            _const_spec((1, d)),
            _const_spec((3, 1, CHUNK)),
            _const_spec((n_gate_chunks, 1, CHUNK)),
            _const_spec((1, CHUNK)),
            _const_spec((1, CHUNK)),
            _const_spec((1, CHUNK)),
        ],
        out_specs=out_specs,
        scratch_shapes=[
            pltpu.VMEM((tm, d), BF16),
            pltpu.VMEM((groups + _halo_groups(CONF_WIDTH), SUBLANES, CHUNK), F32),
            pltpu.VMEM((groups + _halo_groups(SC_WIDTH), SUBLANES, CHUNK), F32),
            pltpu.VMEM((groups, SUBLANES, CHUNK), F32),
        ],
        compiler_params=_compiler_params(1),
        name="in_proj",
    )(x2, w, hmean, cw8, scw8, ng, qkvg, gb, cb, lng, lnb)


def _attn_kernel(q_ref, k_ref, v_ref, o_ref, c_scr, acc_scr):
    s_len = q_ref.shape[0]
    qr, win, top = ATTN_ROWS, ATTN_WINDOW, ATTN_FAR_ROWS
    n_q = s_len // qr
    lane = lax.broadcasted_iota(jnp.int32, (1, LANES), 1)
    head0 = lane < HEAD_DIM
    rows = lax.broadcasted_iota(jnp.int32, (win, win), 0)
    cols = lax.broadcasted_iota(jnp.int32, (win, win), 1)
    tri = (rows >= cols).astype(BF16)
    q_row = lax.broadcasted_iota(jnp.int32, (2 * qr, win), 0) & (qr - 1)
    k_col = lax.broadcasted_iota(jnp.int32, (2 * qr, win), 1)


    def softplus(z):
        neg_abs = lax.bitcast_convert_type(
            lax.bitcast_convert_type(z, jnp.uint32) | jnp.uint32(0x80000000), F32)
        return jnp.maximum(z, 0.0) + jnp.log(1.0 + jnp.exp(neg_abs))

    def rev_cumsum(sp, t):
        return jnp.dot(sp.astype(BF16), t, preferred_element_type=F32)

    def stacked_q(q0, nrows=qr):
        qb = q_ref[pl.ds(q0, nrows), :]
        zero = jnp.zeros((), BF16)
        return jnp.concatenate([jnp.where(head0, qb, zero), jnp.where(head0, zero, qb)], axis=0)

    def merge_heads(pv):
        half = pv.shape[0] // 2
        return jnp.where(head0, pv[:half], pv[half:])

    def split_min(c):
        lo = jnp.minimum(jnp.min(c[:top]), jnp.min(c[qr:qr + top]))
        hi = jnp.minimum(jnp.min(c[top:qr]), jnp.min(c[qr + top:]))
        return lo, hi

    def near_diagonal(j):
        zs, masks, sps, k0s = [], [], [], []
        for t in range(ATTN_Q_PAR):
            q0 = pl.multiple_of((j * ATTN_Q_PAR + t) * qr, qr)
            k0 = pl.multiple_of(jnp.maximum(q0 + qr - win, 0), qr)
            z = lax.dot_general(stacked_q(q0), k_ref[pl.ds(k0, win), :],
                                (((1,), (1,)), ((), ())), preferred_element_type=F32)
            causal = k_col < q_row + (q0 - k0)
            zs.append(z)
            masks.append(causal)
            sps.append(jnp.where(causal, softplus(z), 0.0))
            k0s.append(k0)
        cs_all = rev_cumsum(jnp.concatenate(sps, axis=0), tri)
        cmins = []
        for t in range(ATTN_Q_PAR):
            cs = cs_all[t * 2 * qr:(t + 1) * 2 * qr]
            p = jnp.where(masks[t], jnp.exp(zs[t] - cs), 0.0).astype(BF16)
            c = cs[:, 0:1]
            c_scr[t] = c
            cmins.append(split_min(c))
            pv = jnp.dot(p, v_ref[pl.ds(k0s[t], win), :], preferred_element_type=F32)
            acc_scr[t] = merge_heads(pv)
        return [m[0] for m in cmins], [m[1] for m in cmins]

    def q_group(j, carry):
        cm_top, cm_rest = near_diagonal(j)
        first = j * ATTN_Q_PAR
        q0s = [pl.multiple_of((first + t) * qr, qr) for t in range(ATTN_Q_PAR)]
        tri_far = tri[:qr, :qr]

        def next_key_block(t, step):
            return first + t - win // qr - step

        def pending(step, cm):
            todo = [jnp.logical_and(next_key_block(t, step) >= 0, cm[t] < UNDERFLOW_BOUND)
                    for t in range(ATTN_Q_PAR)]
            return functools.reduce(jnp.logical_or, todo)

        def far_step(step, nrows):
            zs, sps, kfs, valids = [], [], [], []
            for t in range(ATTN_Q_PAR):
                kb = next_key_block(t, step)
                kf = pl.multiple_of(jnp.maximum(kb, 0) * qr, qr)
                z = lax.dot_general(stacked_q(q0s[t], nrows), k_ref[pl.ds(kf, qr), :],
                                    (((1,), (1,)), ((), ())), preferred_element_type=F32)
                zs.append(z)
                sps.append(softplus(z))
                kfs.append(kf)
                valids.append(kb >= 0)
            cs_all = rev_cumsum(jnp.concatenate(sps, axis=0), tri_far)
            cs_new = []
            for t in range(ATTN_Q_PAR):
                cs = cs_all[t * 2 * nrows:(t + 1) * 2 * nrows]
                c_prev = jnp.concatenate([c_scr[t, 0:nrows], c_scr[t, qr:qr + nrows]], axis=0)
                p = jnp.where(valids[t], jnp.exp(zs[t] - cs - c_prev), 0.0).astype(BF16)
                c = c_prev + jnp.where(valids[t], cs[:, 0:1], 0.0)
                c_scr[t, 0:nrows] = c[:nrows]
                c_scr[t, qr:qr + nrows] = c[nrows:]
                acc_scr[t, 0:nrows] += merge_heads(
                    jnp.dot(p, v_ref[pl.ds(kfs[t], qr), :], preferred_element_type=F32))
                cs_new.append(c)
            return cs_new

        def whole_cond(st):
            return pending(st[0], st[1 + ATTN_Q_PAR:])

        def whole_body(st):
            mins = [split_min(c) for c in far_step(st[0], qr)]
            return (st[0] + 1, *[m[0] for m in mins], *[m[1] for m in mins])

        st = lax.while_loop(whole_cond, whole_body, (jnp.int32(0), *cm_top, *cm_rest))

        def top_cond(st):
            return pending(st[0], st[1:])

        def top_body(st):
            return (st[0] + 1, *[jnp.min(c) for c in far_step(st[0], top)])

        lax.while_loop(top_cond, top_body, st[:1 + ATTN_Q_PAR])
        for t in range(ATTN_Q_PAR):
            o_ref[pl.ds(q0s[t], qr), :] = acc_scr[t].astype(o_ref.dtype)
        return carry

    lax.fori_loop(0, n_q // ATTN_Q_PAR, q_group, 0)


def _attention(qkv):
    _, b, s, w = qkv.shape
    n_pairs = w // LANES
    spec = lambda which: pl.BlockSpec((None, None, s, LANES), lambda bi, hp: (which, bi, 0, hp))
    return pl.pallas_call(
        _attn_kernel,
        out_shape=jax.ShapeDtypeStruct((b, s, w), BF16),
        grid=(b, n_pairs),
        in_specs=[spec(0), spec(1), spec(2)],
        out_specs=pl.BlockSpec((None, s, LANES), lambda bi, hp: (bi, 0, hp)),
        scratch_shapes=[
            pltpu.VMEM((ATTN_Q_PAR, 2 * ATTN_ROWS, 1), F32),
            pltpu.VMEM((ATTN_Q_PAR, ATTN_ROWS, LANES), F32),
        ],
        compiler_params=_compiler_params(2),
        name="stickbreak_attn",
    )(qkv, qkv, qkv)


def _post_kernel(x_ref, o_ref, hbn_ref, hc_ref, g_ref,
                 wa_ref, wb_ref, wc_ref, wo_ref, fg_ref, w1_ref, w2_ref,
                 out_ref, *, ffn_chunk):
    d = x_ref.shape[1]
    per_branch = d // CHUNK
    y_a = jnp.dot(o_ref[...], wa_ref[...], preferred_element_type=F32)
    y_b = jnp.dot(hbn_ref[...], wb_ref[...], preferred_element_type=F32)
    y_c = jnp.dot(hc_ref[...], wc_ref[...], preferred_element_type=F32)
    parts = []
    for c in range(per_branch):
        sl = slice(c * CHUNK, (c + 1) * CHUNK)
        gate = lambda j: _sigmoid(g_ref[j].astype(F32))
        parts.append(gate(c) * y_a[:, sl]
                     + gate(per_branch + c) * y_b[:, sl]
                     + gate(2 * per_branch + c) * y_c[:, sl])
    merged = jnp.concatenate(parts, axis=1).astype(BF16)
    x1 = x_ref[...] + jnp.dot(merged, wo_ref[...], preferred_element_type=F32)

    ms = jnp.mean(x1 * x1, axis=-1, keepdims=True)
    xn = (x1 * lax.rsqrt(ms + EPS) * fg_ref[...]).astype(BF16)
    fc = ffn_chunk
    d_ff = w2_ref.shape[0]
    y = x1
    for c in range(d_ff // fc):
        gt = jnp.dot(xn, w1_ref[:, fc * c:fc * (c + 1)], preferred_element_type=F32)
        up = jnp.dot(xn, w1_ref[:, d_ff + fc * c:d_ff + fc * (c + 1)],
                     preferred_element_type=F32)
        act = (gt * _sigmoid(gt) * up).astype(BF16)
        y = y + jnp.dot(act, w2_ref[fc * c:fc * (c + 1), :], preferred_element_type=F32)
    out_ref[...] = y


def _post(x2, o, hbn, hc, g, wa, wb, wc, wo, fg, w1, w2, *, tm, ffn_chunk):
    t, d = x2.shape
    n_gate_chunks = g.shape[0]
    d_ff = w2.shape[0]
    tile = lambda width: pl.BlockSpec((tm, width), lambda i: (i, 0))
    kern = functools.partial(_post_kernel, ffn_chunk=ffn_chunk)
    return pl.pallas_call(
        kern,
        out_shape=jax.ShapeDtypeStruct((t, d), F32),
        grid=(t // tm,),
        in_specs=[
            tile(d), tile(CHUNK), tile(CHUNK), tile(CHUNK),
            pl.BlockSpec((n_gate_chunks, tm, CHUNK), lambda i: (0, i, 0)),
            _const_spec((CHUNK, d)), _const_spec((CHUNK, d)), _const_spec((CHUNK, d)),
            _const_spec((d, d)), _const_spec((1, d)),
            _const_spec((d, 2 * d_ff)), _const_spec((d_ff, d)),
        ],
        out_specs=tile(d),
        compiler_params=_compiler_params(1),
        name="post",
    )(x2, o, hbn, hc, g, wa, wb, wc, wo, fg, w1, w2)


def _pick_tile(n, want):
    t = min(n, want)
    while n % t:
        t //= 2
    return t


def _ffn_chunk(d_ff):
    for c in (512, 256, 128):
        if d_ff % c == 0:
            return c
    raise ValueError(f"d_ff={d_ff} must be a multiple of {LANES}")


def kernel(x, mix_norm_g, w_in, q_norm_g, k_norm_g, w_attn_out, conf_dw_w, conf_dw_b,
           conf_ln_g, conf_ln_b, w_conf_out, sc_conv_w, w_sc_out, gate_b, w_o,
           ffn_norm_g, w_ffn_in, w_ffn_out):
    b, s, d = x.shape
    depth = w_in.shape[0]
    d_ff = w_ffn_out.shape[1]
    n_chunks = w_in.shape[2] // CHUNK
    assert w_in.shape[2] == N_BRANCH_CHUNKS * CHUNK + 3 * d and N_HEADS * HEAD_DIM == CHUNK
    assert d % CHUNK == 0 and s % (ATTN_ROWS * ATTN_Q_PAR) == 0 and s >= ATTN_WINDOW
    tm_in = _pick_tile(s, 512)
    tm_post = _pick_tile(b * s, 512)
    assert tm_in % (CONV_GROUPS * SUBLANES) == 0 and tm_in // SUBLANES >= _halo_groups(CONF_WIDTH)
    fc = _ffn_chunk(d_ff)

    head_id = np.arange(CHUNK) // HEAD_DIM
    hmean = jnp.asarray((head_id[:, None] == head_id[None, :]) / HEAD_DIM, dtype=BF16)
    scale = float(1.0 / np.sqrt(HEAD_DIM))
    sub_bcast = lambda w: jnp.broadcast_to(w[:, None, :], (w.shape[0], SUBLANES, w.shape[1]))

    x2 = x.reshape(b * s, d)
    for l in range(depth):
        qkvg = jnp.stack([jnp.tile(q_norm_g[l], N_HEADS) * scale,
                          jnp.tile(k_norm_g[l], N_HEADS),
                          jnp.ones((CHUNK,), F32)])[:, None, :]
        qkv, hbn, hc, g = _in_proj(
            x2, mix_norm_g[l][None, :], w_in[l].astype(BF16), hmean, qkvg,
            gate_b[l].reshape(n_chunks - N_BRANCH_CHUNKS, 1, CHUNK),
            sub_bcast(conf_dw_w[l]), conf_dw_b[l][None, :],
            conf_ln_g[l][None, :], conf_ln_b[l][None, :], sub_bcast(sc_conv_w[l]),
            tm=tm_in, tiles_per_seq=s // tm_in)
        o = _attention(qkv.reshape(3, b, s, CHUNK)).reshape(b * s, CHUNK)
        x2 = _post(
            x2, o, hbn, hc, g, w_attn_out[l].astype(BF16), w_conf_out[l].astype(BF16),
            w_sc_out[l].astype(BF16), w_o[l].astype(BF16), ffn_norm_g[l][None, :],
            w_ffn_in[l].astype(BF16), w_ffn_out[l].astype(BF16), tm=tm_post, ffn_chunk=fc)
    return x2.reshape(b, s, d)
```

```python
import functools

import jax
import jax.numpy as jnp
import numpy as np
from jax import lax
from jax.experimental import pallas as pl
from jax.experimental.pallas import tpu as pltpu

F32 = jnp.float32
BF16 = jnp.bfloat16

EPS = 1e-6
N_HEADS = 8
HEAD_DIM = 64
CONF_WIDTH = 31
SC_WIDTH = 3
LANES = 128
SUBLANES = 8
CHUNK = 512
N_BRANCH_CHUNKS = 8
CONV_GROUPS = 8
ATTN_ROWS = 128
ATTN_WINDOW = 256
ATTN_Q_PAR = 16
ATTN_FAR_ROWS = 64
UNDERFLOW_BOUND = 88.0
VMEM_LIMIT_BYTES = 56 * 1024 * 1024


def _compiler_params(n_grid_axes):
    return pltpu.CompilerParams(
        dimension_semantics=("arbitrary",) * n_grid_axes, vmem_limit_bytes=VMEM_LIMIT_BYTES)


def _const_spec(shape):
    nd = len(shape)
    return pl.BlockSpec(shape, lambda *_: (0,) * nd, pipeline_mode=pl.Buffered(1))


def _sigmoid(x):
    return 0.5 * jnp.tanh(0.5 * x) + 0.5


def _halo_groups(width):
    return -(-(width - 1) // SUBLANES)


def _causal_conv_rows(ext_ref, w8_ref, width, g_out, n_groups):
    halo = _halo_groups(width)
    sub = lax.broadcasted_iota(jnp.int32, (1, SUBLANES, CHUNK), 1)
    base = g_out + halo
    y = None
    for b in range(min(SUBLANES, width)):
        n_a = (width - 1 - b) // SUBLANES + 1
        lo = 0 if b else 1
        u = None
        for a in range(n_a):
            tap = w8_ref[width - 1 - (SUBLANES * a + b)]
            t = tap * ext_ref[pl.ds(base - 1 + lo - a, n_groups + 1 - lo)]
            u = t if u is None else u + t
        if b == 0:
            y = u
        else:
            r = pltpu.roll(u, b, axis=1)
            y = y + jnp.where(sub >= b, r[1:], r[:-1])
    return y


def _in_proj_kernel(x_ref, w_ref, hmean_ref, cw_ref, scw_ref,
                    ng_ref, qkvg_ref, gb_ref, cb_ref, lng_ref, lnb_ref,
                    qkv_out, hbn_out, hc_out, g_out,
                    u_scr, hbe_scr, me_scr, bg_scr, *, tiles_per_seq):
    tm = x_ref.shape[0]
    groups = tm // SUBLANES
    conf_halo = _halo_groups(CONF_WIDTH)
    sc_halo = _halo_groups(SC_WIDTH)
    n_gate_chunks = g_out.shape[0]
    conv_rows = CONV_GROUPS * SUBLANES
    n_conv = groups // CONV_GROUPS

    @pl.when(pl.program_id(0) % tiles_per_seq == 0)
    def _():
        hbe_scr[0:conf_halo] = jnp.zeros((conf_halo, SUBLANES, CHUNK), F32)
        me_scr[0:sc_halo] = jnp.zeros((sc_halo, SUBLANES, CHUNK), F32)

    x = x_ref[...]
    ms = jnp.mean(x * x, axis=-1, keepdims=True)
    u_scr[...] = (x * lax.rsqrt(ms + EPS) * ng_ref[...]).astype(BF16)

    def proj(c):
        return jnp.dot(u_scr[...], w_ref[:, c * CHUNK:(c + 1) * CHUNK],
                       preferred_element_type=F32)

    def grouped(a):
        return a.reshape(groups, SUBLANES, CHUNK)

    cf_val = proj(3)
    hbe_scr[conf_halo:] = grouped(cf_val * _sigmoid(proj(4)))

    def conf_step(c):
        rows = pl.ds(pl.multiple_of(c * conv_rows, conv_rows), conv_rows)
        acc = _causal_conv_rows(hbe_scr, cw_ref, CONF_WIDTH, c * CONV_GROUPS, CONV_GROUPS)
        acc = acc.reshape(conv_rows, CHUNK) + cb_ref[...]
        mu = jnp.mean(acc, axis=-1, keepdims=True)
        cen = acc - mu
        var = jnp.mean(cen * cen, axis=-1, keepdims=True)
        y = cen * lax.rsqrt(var + EPS) * lng_ref[...] + lnb_ref[...]
        hbn_out[rows, :] = (y * _sigmoid(y)).astype(BF16)

    def sc_step(c):
        rows = pl.ds(pl.multiple_of(c * conv_rows, conv_rows), conv_rows)
        sc = _causal_conv_rows(me_scr, scw_ref, SC_WIDTH, c * CONV_GROUPS, CONV_GROUPS)
        hc = bg_scr[pl.ds(c * CONV_GROUPS, CONV_GROUPS)] * sc
        hc_out[rows, :] = hc.reshape(conv_rows, CHUNK).astype(BF16)

    sc_x = proj(5)
    bg_scr[...] = grouped(proj(6))
    me_scr[sc_halo:] = grouped(proj(7) * sc_x)

    for j in range(n_gate_chunks):
        g_out[j] = (proj(N_BRANCH_CHUNKS + j) + gb_ref[j]).astype(BF16)
    for i in range(3):
        a = proj(i)
        if i < 2:
            msq = jnp.dot((a * a).astype(BF16), hmean_ref[...], preferred_element_type=F32)
            a = a * lax.rsqrt(msq + EPS)
        qkv_out[i] = (a * qkvg_ref[i]).astype(BF16)

    steps_per_iter = 2 if n_conv % 2 == 0 else 1

    def conv_iter(i, carry):
        for s in range(steps_per_iter):
            conf_step(i * steps_per_iter + s)
            sc_step(i * steps_per_iter + s)
        return carry

    lax.fori_loop(0, n_conv // steps_per_iter, conv_iter, 0)

    hbe_scr[0:conf_halo] = hbe_scr[groups:groups + conf_halo]
    me_scr[0:sc_halo] = me_scr[groups:groups + sc_halo]


def _in_proj(x2, ng, w, hmean, qkvg, gb, cw8, cb, lng, lnb, scw8, *, tm, tiles_per_seq):
    t, d = x2.shape
    n_chunks = w.shape[1] // CHUNK
    n_gate_chunks = n_chunks - N_BRANCH_CHUNKS
    groups = tm // SUBLANES
    row = lambda i: (i, 0)
    stacked = lambda i: (0, i, 0)
    outs = [
        jax.ShapeDtypeStruct((3, t, CHUNK), BF16),
        jax.ShapeDtypeStruct((t, CHUNK), BF16),
        jax.ShapeDtypeStruct((t, CHUNK), BF16),
        jax.ShapeDtypeStruct((n_gate_chunks, t, CHUNK), BF16),
    ]
    out_specs = [
        pl.BlockSpec((3, tm, CHUNK), stacked),
        pl.BlockSpec((tm, CHUNK), row),
        pl.BlockSpec((tm, CHUNK), row),
        pl.BlockSpec((n_gate_chunks, tm, CHUNK), stacked),
    ]
    kern = functools.partial(_in_proj_kernel, tiles_per_seq=tiles_per_seq)
    return pl.pallas_call(
        kern,
        out_shape=outs,
        grid=(t // tm,),
        in_specs=[
            pl.BlockSpec((tm, d), row),
            _const_spec((d, n_chunks * CHUNK)),
            _const_spec((CHUNK, CHUNK)),
            _const_spec((CONF_WIDTH, SUBLANES, CHUNK)),
            _const_spec((SC_WIDTH, SUBLANES, CHUNK)),
            _const_spec((1, d)),
            _const_spec((3, 1, CHUNK)),
            _const_spec((n_gate_chunks, 1, CHUNK)),
            _const_spec((1, CHUNK)),
            _const_spec((1, CHUNK)),
            _const_spec((1, CHUNK)),
        ],
        out_specs=out_specs,
        scratch_shapes=[
            pltpu.VMEM((tm, d), BF16),
            pltpu.VMEM((groups + _halo_groups(CONF_WIDTH), SUBLANES, CHUNK), F32),
            pltpu.VMEM((groups + _halo_groups(SC_WIDTH), SUBLANES, CHUNK), F32),
            pltpu.VMEM((groups, SUBLANES, CHUNK), F32),
        ],
        compiler_params=_compiler_params(1),
        name="in_proj",
    )(x2, w, hmean, cw8, scw8, ng, qkvg, gb, cb, lng, lnb)


def _attn_kernel(q_ref, k_ref, v_ref, o_ref, c_scr, acc_scr):
    s_len = q_ref.shape[0]
    qr, win, top = ATTN_ROWS, ATTN_WINDOW, ATTN_FAR_ROWS
    n_q = s_len // qr
    lane = lax.broadcasted_iota(jnp.int32, (1, LANES), 1)
    head0 = lane < HEAD_DIM
    rows = lax.broadcasted_iota(jnp.int32, (win, win), 0)
    cols = lax.broadcasted_iota(jnp.int32, (win, win), 1)
    tri = (rows >= cols).astype(BF16)
    q_row = lax.broadcasted_iota(jnp.int32, (2 * qr, win), 0) & (qr - 1)
    k_col = lax.broadcasted_iota(jnp.int32, (2 * qr, win), 1)


    def softplus(z):
        neg_abs = lax.bitcast_convert_type(
            lax.bitcast_convert_type(z, jnp.uint32) | jnp.uint32(0x80000000), F32)
        return jnp.maximum(z, 0.0) + jnp.log(1.0 + jnp.exp(neg_abs))

    def rev_cumsum(sp, t):
        return jnp.dot(sp.astype(BF16), t, preferred_element_type=F32)

    def stacked_q(q0, nrows=qr):
        qb = q_ref[pl.ds(q0, nrows), :]
        zero = jnp.zeros((), BF16)
        return jnp.concatenate([jnp.where(head0, qb, zero), jnp.where(head0, zero, qb)], axis=0)

    def merge_heads(pv):
        half = pv.shape[0] // 2
        return jnp.where(head0, pv[:half], pv[half:])

    def split_min(c):
        lo = jnp.minimum(jnp.min(c[:top]), jnp.min(c[qr:qr + top]))
        hi = jnp.minimum(jnp.min(c[top:qr]), jnp.min(c[qr + top:]))
        return lo, hi

    def near_diagonal(j):
        zs, masks, sps, k0s = [], [], [], []
        for t in range(ATTN_Q_PAR):
            q0 = pl.multiple_of((j * ATTN_Q_PAR + t) * qr, qr)
            k0 = pl.multiple_of(jnp.maximum(q0 + qr - win, 0), qr)
            z = lax.dot_general(stacked_q(q0), k_ref[pl.ds(k0, win), :],
                                (((1,), (1,)), ((), ())), preferred_element_type=F32)
            causal = k_col < q_row + (q0 - k0)
            zs.append(z)
            masks.append(causal)
            sps.append(jnp.where(causal, softplus(z), 0.0))
            k0s.append(k0)
        cs_all = rev_cumsum(jnp.concatenate(sps, axis=0), tri)
        cmins = []
        for t in range(ATTN_Q_PAR):
            cs = cs_all[t * 2 * qr:(t + 1) * 2 * qr]
            p = jnp.where(masks[t], jnp.exp(zs[t] - cs), 0.0).astype(BF16)
            c = cs[:, 0:1]
            c_scr[t] = c
            cmins.append(split_min(c))
            pv = jnp.dot(p, v_ref[pl.ds(k0s[t], win), :], preferred_element_type=F32)
            acc_scr[t] = merge_heads(pv)
        return [m[0] for m in cmins], [m[1] for m in cmins]

    def q_group(j, carry):
        cm_top, cm_rest = near_diagonal(j)
        first = j * ATTN_Q_PAR
        q0s = [pl.multiple_of((first + t) * qr, qr) for t in range(ATTN_Q_PAR)]
        tri_far = tri[:qr, :qr]

        def next_key_block(t, step):
            return first + t - win // qr - step

        def pending(step, cm):
            todo = [jnp.logical_and(next_key_block(t, step) >= 0, cm[t] < UNDERFLOW_BOUND)
                    for t in range(ATTN_Q_PAR)]
            return functools.reduce(jnp.logical_or, todo)

        def far_step(step, nrows):
            zs, sps, kfs, valids = [], [], [], []
            for t in range(ATTN_Q_PAR):
                kb = next_key_block(t, step)
                kf = pl.multiple_of(jnp.maximum(kb, 0) * qr, qr)
                z = lax.dot_general(stacked_q(q0s[t], nrows), k_ref[pl.ds(kf, qr), :],
                                    (((1,), (1,)), ((), ())), preferred_element_type=F32)
                zs.append(z)
                sps.append(softplus(z))
                kfs.append(kf)
                valids.append(kb >= 0)
            cs_all = rev_cumsum(jnp.concatenate(sps, axis=0), tri_far)
            cs_new = []
            for t in range(ATTN_Q_PAR):
                cs = cs_all[t * 2 * nrows:(t + 1) * 2 * nrows]
                c_prev = jnp.concatenate([c_scr[t, 0:nrows], c_scr[t, qr:qr + nrows]], axis=0)
                p = jnp.where(valids[t], jnp.exp(zs[t] - cs - c_prev), 0.0).astype(BF16)
                c = c_prev + jnp.where(valids[t], cs[:, 0:1], 0.0)
                c_scr[t, 0:nrows] = c[:nrows]
                c_scr[t, qr:qr + nrows] = c[nrows:]
                acc_scr[t, 0:nrows] += merge_heads(
                    jnp.dot(p, v_ref[pl.ds(kfs[t], qr), :], preferred_element_type=F32))
                cs_new.append(c)
            return cs_new

        def whole_cond(st):
            return pending(st[0], st[1 + ATTN_Q_PAR:])

        def whole_body(st):
            mins = [split_min(c) for c in far_step(st[0], qr)]
            return (st[0] + 1, *[m[0] for m in mins], *[m[1] for m in mins])

        st = lax.while_loop(whole_cond, whole_body, (jnp.int32(0), *cm_top, *cm_rest))

        def top_cond(st):
            return pending(st[0], st[1:])

        def top_body(st):
            return (st[0] + 1, *[jnp.min(c) for c in far_step(st[0], top)])

        lax.while_loop(top_cond, top_body, st[:1 + ATTN_Q_PAR])
        for t in range(ATTN_Q_PAR):
            o_ref[pl.ds(q0s[t], qr), :] = acc_scr[t].astype(o_ref.dtype)
        return carry

    lax.fori_loop(0, n_q // ATTN_Q_PAR, q_group, 0)


def _attention(qkv):
    _, b, s, w = qkv.shape
    n_pairs = w // LANES
    spec = lambda which: pl.BlockSpec((None, None, s, LANES), lambda bi, hp: (which, bi, 0, hp))
    return pl.pallas_call(
        _attn_kernel,
        out_shape=jax.ShapeDtypeStruct((b, s, w), BF16),
        grid=(b, n_pairs),
        in_specs=[spec(0), spec(1), spec(2)],
        out_specs=pl.BlockSpec((None, s, LANES), lambda bi, hp: (bi, 0, hp)),
        scratch_shapes=[
            pltpu.VMEM((ATTN_Q_PAR, 2 * ATTN_ROWS, 1), F32),
            pltpu.VMEM((ATTN_Q_PAR, ATTN_ROWS, LANES), F32),
        ],
        compiler_params=_compiler_params(2),
        name="stickbreak_attn",
    )(qkv, qkv, qkv)


def _post_kernel(x_ref, o_ref, hbn_ref, hc_ref, g_ref,
                 wa_ref, wb_ref, wc_ref, wo_ref, fg_ref, w1_ref, w2_ref,
                 out_ref, *, ffn_chunk):
    d = x_ref.shape[1]
    per_branch = d // CHUNK
    y_a = jnp.dot(o_ref[...], wa_ref[...], preferred_element_type=F32)
    y_b = jnp.dot(hbn_ref[...], wb_ref[...], preferred_element_type=F32)
    y_c = jnp.dot(hc_ref[...], wc_ref[...], preferred_element_type=F32)
    parts = []
    for c in range(per_branch):
        sl = slice(c * CHUNK, (c + 1) * CHUNK)
        gate = lambda j: _sigmoid(g_ref[j].astype(F32))
        parts.append(gate(c) * y_a[:, sl]
                     + gate(per_branch + c) * y_b[:, sl]
                     + gate(2 * per_branch + c) * y_c[:, sl])
    merged = jnp.concatenate(parts, axis=1).astype(BF16)
    x1 = x_ref[...] + jnp.dot(merged, wo_ref[...], preferred_element_type=F32)

    ms = jnp.mean(x1 * x1, axis=-1, keepdims=True)
    xn = (x1 * lax.rsqrt(ms + EPS) * fg_ref[...]).astype(BF16)
    fc = ffn_chunk
    d_ff = w2_ref.shape[0]
    y = x1
    for c in range(d_ff // fc):
        gt = jnp.dot(xn, w1_ref[:, fc * c:fc * (c + 1)], preferred_element_type=F32)
        up = jnp.dot(xn, w1_ref[:, d_ff + fc * c:d_ff + fc * (c + 1)],
                     preferred_element_type=F32)
        act = (gt * _sigmoid(gt) * up).astype(BF16)
        y = y + jnp.dot(act, w2_ref[fc * c:fc * (c + 1), :], preferred_element_type=F32)
    out_ref[...] = y


def _post(x2, o, hbn, hc, g, wa, wb, wc, wo, fg, w1, w2, *, tm, ffn_chunk):
    t, d = x2.shape
    n_gate_chunks = g.shape[0]
    d_ff = w2.shape[0]
    tile = lambda width: pl.BlockSpec((tm, width), lambda i: (i, 0))
    kern = functools.partial(_post_kernel, ffn_chunk=ffn_chunk)
    return pl.pallas_call(
        kern,
        out_shape=jax.ShapeDtypeStruct((t, d), F32),
        grid=(t // tm,),
        in_specs=[
            tile(d), tile(CHUNK), tile(CHUNK), tile(CHUNK),
            pl.BlockSpec((n_gate_chunks, tm, CHUNK), lambda i: (0, i, 0)),
            _const_spec((CHUNK, d)), _const_spec((CHUNK, d)), _const_spec((CHUNK, d)),
            _const_spec((d, d)), _const_spec((1, d)),
            _const_spec((d, 2 * d_ff)), _const_spec((d_ff, d)),
        ],
        out_specs=tile(d),
        compiler_params=_compiler_params(1),
        name="post",
    )(x2, o, hbn, hc, g, wa, wb, wc, wo, fg, w1, w2)


def _pick_tile(n, want):
    t = min(n, want)
    while n % t:
        t //= 2
    return t


def _ffn_chunk(d_ff):
    for c in (512, 256, 128):
        if d_ff % c == 0:
            return c
    raise ValueError(f"d_ff={d_ff} must be a multiple of {LANES}")


def kernel(x, mix_norm_g, w_in, q_norm_g, k_norm_g, w_attn_out, conf_dw_w, conf_dw_b,
           conf_ln_g, conf_ln_b, w_conf_out, sc_conv_w, w_sc_out, gate_b, w_o,
           ffn_norm_g, w_ffn_in, w_ffn_out):
    b, s, d = x.shape
    depth = w_in.shape[0]
    d_ff = w_ffn_out.shape[1]
    n_chunks = w_in.shape[2] // CHUNK
    assert w_in.shape[2] == N_BRANCH_CHUNKS * CHUNK + 3 * d and N_HEADS * HEAD_DIM == CHUNK
    assert d % CHUNK == 0 and s % (ATTN_ROWS * ATTN_Q_PAR) == 0 and s >= ATTN_WINDOW
    tm_in = _pick_tile(s, 512)
    tm_post = _pick_tile(b * s, 512)
    assert tm_in % (CONV_GROUPS * SUBLANES) == 0 and tm_in // SUBLANES >= _halo_groups(CONF_WIDTH)
    fc = _ffn_chunk(d_ff)

    head_id = np.arange(CHUNK) // HEAD_DIM
    hmean = jnp.asarray((head_id[:, None] == head_id[None, :]) / HEAD_DIM, dtype=BF16)
    scale = float(1.0 / np.sqrt(HEAD_DIM))
    sub_bcast = lambda w: jnp.broadcast_to(w[:, None, :], (w.shape[0], SUBLANES, w.shape[1]))

    x2 = x.reshape(b * s, d)
    for l in range(depth):
        qkvg = jnp.stack([jnp.tile(q_norm_g[l], N_HEADS) * scale,
                          jnp.tile(k_norm_g[l], N_HEADS),
                          jnp.ones((CHUNK,), F32)])[:, None, :]
        qkv, hbn, hc, g = _in_proj(
            x2, mix_norm_g[l][None, :], w_in[l].astype(BF16), hmean, qkvg,
            gate_b[l].reshape(n_chunks - N_BRANCH_CHUNKS, 1, CHUNK),
            sub_bcast(conf_dw_w[l]), conf_dw_b[l][None, :],
            conf_ln_g[l][None, :], conf_ln_b[l][None, :], sub_bcast(sc_conv_w[l]),
            tm=tm_in, tiles_per_seq=s // tm_in)
        o = _attention(qkv.reshape(3, b, s, CHUNK)).reshape(b * s, CHUNK)
        x2 = _post(
            x2, o, hbn, hc, g, w_attn_out[l].astype(BF16), w_conf_out[l].astype(BF16),
            w_sc_out[l].astype(BF16), w_o[l].astype(BF16), ffn_norm_g[l][None, :],
            w_ffn_in[l].astype(BF16), w_ffn_out[l].astype(BF16), tm=tm_post, ffn_chunk=fc)
    return x2.reshape(b, s, d)
```

```python
import functools

import jax
import jax.numpy as jnp
import numpy as np
from jax import lax
from jax.experimental import pallas as pl
from jax.experimental.pallas import tpu as pltpu

F32 = jnp.float32
BF16 = jnp.bfloat16

EPS = 1e-6
N_HEADS = 8
HEAD_DIM = 64
CONF_WIDTH = 31
SC_WIDTH = 3
LANES = 128
SUBLANES = 8
CHUNK = 512
N_BRANCH_CHUNKS = 8
CONV_GROUPS = 8
ATTN_ROWS = 128
ATTN_WINDOW = 256
ATTN_Q_PAR = 16
ATTN_FAR_ROWS = 32
UNDERFLOW_BOUND = 88.0
VMEM_LIMIT_BYTES = 56 * 1024 * 1024


def _compiler_params(n_grid_axes):
    return pltpu.CompilerParams(
        dimension_semantics=("arbitrary",) * n_grid_axes, vmem_limit_bytes=VMEM_LIMIT_BYTES)


def _const_spec(shape):
    nd = len(shape)
    return pl.BlockSpec(shape, lambda *_: (0,) * nd, pipeline_mode=pl.Buffered(1))


def _sigmoid(x):
    return 0.5 * jnp.tanh(0.5 * x) + 0.5


def _halo_groups(width):
    return -(-(width - 1) // SUBLANES)


def _causal_conv_rows(ext_ref, w8_ref, width, g_out, n_groups):
    halo = _halo_groups(width)
    sub = lax.broadcasted_iota(jnp.int32, (1, SUBLANES, CHUNK), 1)
    base = g_out + halo
    y = None
    for b in range(min(SUBLANES, width)):
        n_a = (width - 1 - b) // SUBLANES + 1
        lo = 0 if b else 1
        u = None
        for a in range(n_a):
            tap = w8_ref[width - 1 - (SUBLANES * a + b)]
            t = tap * ext_ref[pl.ds(base - 1 + lo - a, n_groups + 1 - lo)]
            u = t if u is None else u + t
        if b == 0:
            y = u
        else:
            r = pltpu.roll(u, b, axis=1)
            y = y + jnp.where(sub >= b, r[1:], r[:-1])
    return y


def _in_proj_kernel(x_ref, w_ref, hmean_ref, cw_ref, scw_ref,
                    ng_ref, qkvg_ref, gb_ref, cb_ref, lng_ref, lnb_ref,
                    qkv_out, hbn_out, hc_out, g_out,
                    u_scr, hbe_scr, me_scr, bg_scr, *, tiles_per_seq):
    tm = x_ref.shape[0]
    groups = tm // SUBLANES
    conf_halo = _halo_groups(CONF_WIDTH)
    sc_halo = _halo_groups(SC_WIDTH)
    n_gate_chunks = g_out.shape[0]
    conv_rows = CONV_GROUPS * SUBLANES
    n_conv = groups // CONV_GROUPS

    @pl.when(pl.program_id(0) % tiles_per_seq == 0)
    def _():
        hbe_scr[0:conf_halo] = jnp.zeros((conf_halo, SUBLANES, CHUNK), F32)
        me_scr[0:sc_halo] = jnp.zeros((sc_halo, SUBLANES, CHUNK), F32)

    x = x_ref[...]
    ms = jnp.mean(x * x, axis=-1, keepdims=True)
    u_scr[...] = (x * lax.rsqrt(ms + EPS) * ng_ref[...]).astype(BF16)

    def proj(c):
        return jnp.dot(u_scr[...], w_ref[:, c * CHUNK:(c + 1) * CHUNK],
                       preferred_element_type=F32)

    def grouped(a):
        return a.reshape(groups, SUBLANES, CHUNK)

    cf_val = proj(3)
    hbe_scr[conf_halo:] = grouped(cf_val * _sigmoid(proj(4)))

    def conf_step(c):
        rows = pl.ds(pl.multiple_of(c * conv_rows, conv_rows), conv_rows)
        acc = _causal_conv_rows(hbe_scr, cw_ref, CONF_WIDTH, c * CONV_GROUPS, CONV_GROUPS)
        acc = acc.reshape(conv_rows, CHUNK) + cb_ref[...]
        mu = jnp.mean(acc, axis=-1, keepdims=True)
        cen = acc - mu
        var = jnp.mean(cen * cen, axis=-1, keepdims=True)
        y = cen * lax.rsqrt(var + EPS) * lng_ref[...] + lnb_ref[...]
        hbn_out[rows, :] = (y * _sigmoid(y)).astype(BF16)

    def sc_step(c):
        rows = pl.ds(pl.multiple_of(c * conv_rows, conv_rows), conv_rows)
        sc = _causal_conv_rows(me_scr, scw_ref, SC_WIDTH, c * CONV_GROUPS, CONV_GROUPS)
        hc = bg_scr[pl.ds(c * CONV_GROUPS, CONV_GROUPS)] * sc
        hc_out[rows, :] = hc.reshape(conv_rows, CHUNK).astype(BF16)

    sc_x = proj(5)
    bg_scr[...] = grouped(proj(6))
    me_scr[sc_halo:] = grouped(proj(7) * sc_x)

    for j in range(n_gate_chunks):
        g_out[j] = (proj(N_BRANCH_CHUNKS + j) + gb_ref[j]).astype(BF16)
    for i in range(3):
        a = proj(i)
        if i < 2:
            msq = jnp.dot((a * a).astype(BF16), hmean_ref[...], preferred_element_type=F32)
            a = a * lax.rsqrt(msq + EPS)
        qkv_out[i] = (a * qkvg_ref[i]).astype(BF16)

    steps_per_iter = 2 if n_conv % 2 == 0 else 1

    def conv_iter(i, carry):
        for s in range(steps_per_iter):
            conf_step(i * steps_per_iter + s)
            sc_step(i * steps_per_iter + s)
        return carry

    lax.fori_loop(0, n_conv // steps_per_iter, conv_iter, 0)

    hbe_scr[0:conf_halo] = hbe_scr[groups:groups + conf_halo]
    me_scr[0:sc_halo] = me_scr[groups:groups + sc_halo]


def _in_proj(x2, ng, w, hmean, qkvg, gb, cw8, cb, lng, lnb, scw8, *, tm, tiles_per_seq):
    t, d = x2.shape
    n_chunks = w.shape[1] // CHUNK
    n_gate_chunks = n_chunks - N_BRANCH_CHUNKS
    groups = tm // SUBLANES
    row = lambda i: (i, 0)
    stacked = lambda i: (0, i, 0)
    outs = [
        jax.ShapeDtypeStruct((3, t, CHUNK), BF16),
        jax.ShapeDtypeStruct((t, CHUNK), BF16),
        jax.ShapeDtypeStruct((t, CHUNK), BF16),
        jax.ShapeDtypeStruct((n_gate_chunks, t, CHUNK), BF16),
    ]
    out_specs = [
        pl.BlockSpec((3, tm, CHUNK), stacked),
        pl.BlockSpec((tm, CHUNK), row),
        pl.BlockSpec((tm, CHUNK), row),
        pl.BlockSpec((n_gate_chunks, tm, CHUNK), stacked),
    ]
    kern = functools.partial(_in_proj_kernel, tiles_per_seq=tiles_per_seq)
    return pl.pallas_call(
        kern,
        out_shape=outs,
        grid=(t // tm,),
        in_specs=[
            pl.BlockSpec((tm, d), row),
            _const_spec((d, n_chunks * CHUNK)),
            _const_spec((CHUNK, CHUNK)),
            _const_spec((CONF_WIDTH, SUBLANES, CHUNK)),
            _const_spec((SC_WIDTH, SUBLANES, CHUNK)),
            _const_spec((1, d)),
            _const_spec((3, 1, CHUNK)),
            _const_spec((n_gate_chunks, 1, CHUNK)),
            _const_spec((1, CHUNK)),
            _const_spec((1, CHUNK)),
            _const_spec((1, CHUNK)),
        ],
        out_specs=out_specs,
        scratch_shapes=[
            pltpu.VMEM((tm, d), BF16),
            pltpu.VMEM((groups + _halo_groups(CONF_WIDTH), SUBLANES, CHUNK), F32),
            pltpu.VMEM((groups + _halo_groups(SC_WIDTH), SUBLANES, CHUNK), F32),
            pltpu.VMEM((groups, SUBLANES, CHUNK), F32),
        ],
        compiler_params=_compiler_params(1),
        name="in_proj",
    )(x2, w, hmean, cw8, scw8, ng, qkvg, gb, cb, lng, lnb)


def _attn_kernel(q_ref, k_ref, v_ref, o_ref, c_scr, acc_scr):
    s_len = q_ref.shape[0]
    qr, win, top = ATTN_ROWS, ATTN_WINDOW, ATTN_FAR_ROWS
    n_q = s_len // qr
    lane = lax.broadcasted_iota(jnp.int32, (1, LANES), 1)
    head0 = lane < HEAD_DIM
    rows = lax.broadcasted_iota(jnp.int32, (win, win), 0)
    cols = lax.broadcasted_iota(jnp.int32, (win, win), 1)
    tri = (rows >= cols).astype(BF16)
    q_row = lax.broadcasted_iota(jnp.int32, (2 * qr, win), 0) & (qr - 1)
    k_col = lax.broadcasted_iota(jnp.int32, (2 * qr, win), 1)


    def softplus(z):
        neg_abs = lax.bitcast_convert_type(
            lax.bitcast_convert_type(z, jnp.uint32) | jnp.uint32(0x80000000), F32)
        return jnp.maximum(z, 0.0) + jnp.log(1.0 + jnp.exp(neg_abs))

    def rev_cumsum(sp, t):
        return jnp.dot(sp.astype(BF16), t, preferred_element_type=F32)

    def stacked_q(q0, nrows=qr):
        qb = q_ref[pl.ds(q0, nrows), :]
        zero = jnp.zeros((), BF16)
        return jnp.concatenate([jnp.where(head0, qb, zero), jnp.where(head0, zero, qb)], axis=0)

    def merge_heads(pv):
        half = pv.shape[0] // 2
        return jnp.where(head0, pv[:half], pv[half:])

    def split_min(c):
        lo = jnp.minimum(jnp.min(c[:top]), jnp.min(c[qr:qr + top]))
        hi = jnp.minimum(jnp.min(c[top:qr]), jnp.min(c[qr + top:]))
        return lo, hi

    def near_diagonal(j):
        zs, masks, sps, k0s = [], [], [], []
        for t in range(ATTN_Q_PAR):
            q0 = pl.multiple_of((j * ATTN_Q_PAR + t) * qr, qr)
            k0 = pl.multiple_of(jnp.maximum(q0 + qr - win, 0), qr)
            z = lax.dot_general(stacked_q(q0), k_ref[pl.ds(k0, win), :],
                                (((1,), (1,)), ((), ())), preferred_element_type=F32)
            causal = k_col < q_row + (q0 - k0)
            zs.append(z)
            masks.append(causal)
            sps.append(jnp.where(causal, softplus(z), 0.0))
            k0s.append(k0)
        cs_all = rev_cumsum(jnp.concatenate(sps, axis=0), tri)
        cmins = []
        for t in range(ATTN_Q_PAR):
            cs = cs_all[t * 2 * qr:(t + 1) * 2 * qr]
            p = jnp.where(masks[t], jnp.exp(zs[t] - cs), 0.0).astype(BF16)
            c = cs[:, 0:1]
            c_scr[t] = c
            cmins.append(split_min(c))
            pv = jnp.dot(p, v_ref[pl.ds(k0s[t], win), :], preferred_element_type=F32)
            acc_scr[t] = merge_heads(pv)
        return [m[0] for m in cmins], [m[1] for m in cmins]

    def q_group(j, carry):
        cm_top, cm_rest = near_diagonal(j)
        first = j * ATTN_Q_PAR
        q0s = [pl.multiple_of((first + t) * qr, qr) for t in range(ATTN_Q_PAR)]
        tri_far = tri[:qr, :qr]

        def next_key_block(t, step):
            return first + t - win // qr - step

        def pending(step, cm):
            todo = [jnp.logical_and(next_key_block(t, step) >= 0, cm[t] < UNDERFLOW_BOUND)
                    for t in range(ATTN_Q_PAR)]
            return functools.reduce(jnp.logical_or, todo)

        def far_step(step, nrows):
            zs, sps, kfs, valids = [], [], [], []
            for t in range(ATTN_Q_PAR):
                kb = next_key_block(t, step)
                kf = pl.multiple_of(jnp.maximum(kb, 0) * qr, qr)
                z = lax.dot_general(stacked_q(q0s[t], nrows), k_ref[pl.ds(kf, qr), :],
                                    (((1,), (1,)), ((), ())), preferred_element_type=F32)
                zs.append(z)
                sps.append(softplus(z))
                kfs.append(kf)
                valids.append(kb >= 0)
            cs_all = rev_cumsum(jnp.concatenate(sps, axis=0), tri_far)
            cs_new = []
            for t in range(ATTN_Q_PAR):
                cs = cs_all[t * 2 * nrows:(t + 1) * 2 * nrows]
                c_prev = jnp.concatenate([c_scr[t, 0:nrows], c_scr[t, qr:qr + nrows]], axis=0)
                p = jnp.where(valids[t], jnp.exp(zs[t] - cs - c_prev), 0.0).astype(BF16)
                c = c_prev + jnp.where(valids[t], cs[:, 0:1], 0.0)
                c_scr[t, 0:nrows] = c[:nrows]
                c_scr[t, qr:qr + nrows] = c[nrows:]
                acc_scr[t, 0:nrows] += merge_heads(
                    jnp.dot(p, v_ref[pl.ds(kfs[t], qr), :], preferred_element_type=F32))
                cs_new.append(c)
            return cs_new

        def whole_cond(st):
            return pending(st[0], st[1 + ATTN_Q_PAR:])

        def whole_body(st):
            mins = [split_min(c) for c in far_step(st[0], qr)]
            return (st[0] + 1, *[m[0] for m in mins], *[m[1] for m in mins])

        st = lax.while_loop(whole_cond, whole_body, (jnp.int32(0), *cm_top, *cm_rest))

        def top_cond(st):
            return pending(st[0], st[1:])

        def top_body(st):
            return (st[0] + 1, *[jnp.min(c) for c in far_step(st[0], top)])

        lax.while_loop(top_cond, top_body, st[:1 + ATTN_Q_PAR])
        for t in range(ATTN_Q_PAR):
            o_ref[pl.ds(q0s[t], qr), :] = acc_scr[t].astype(o_ref.dtype)
        return carry

    lax.fori_loop(0, n_q // ATTN_Q_PAR, q_group, 0)


def _attention(qkv):
    _, b, s, w = qkv.shape
    n_pairs = w // LANES
    spec = lambda which: pl.BlockSpec((None, None, s, LANES), lambda bi, hp: (which, bi, 0, hp))
    return pl.pallas_call(
        _attn_kernel,
        out_shape=jax.ShapeDtypeStruct((b, s, w), BF16),
        grid=(b, n_pairs),
        in_specs=[spec(0), spec(1), spec(2)],
        out_specs=pl.BlockSpec((None, s, LANES), lambda bi, hp: (bi, 0, hp)),
        scratch_shapes=[
            pltpu.VMEM((ATTN_Q_PAR, 2 * ATTN_ROWS, 1), F32),
            pltpu.VMEM((ATTN_Q_PAR, ATTN_ROWS, LANES), F32),
        ],
        compiler_params=_compiler_params(2),
        name="stickbreak_attn",
    )(qkv, qkv, qkv)


def _post_kernel(x_ref, o_ref, hbn_ref, hc_ref, g_ref,
                 wa_ref, wb_ref, wc_ref, wo_ref, fg_ref, w1_ref, w2_ref,
                 out_ref, *, ffn_chunk):
    d = x_ref.shape[1]
    per_branch = d // CHUNK
    y_a = jnp.dot(o_ref[...], wa_ref[...], preferred_element_type=F32)
    y_b = jnp.dot(hbn_ref[...], wb_ref[...], preferred_element_type=F32)
    y_c = jnp.dot(hc_ref[...], wc_ref[...], preferred_element_type=F32)
    parts = []
    for c in range(per_branch):
        sl = slice(c * CHUNK, (c + 1) * CHUNK)
        gate = lambda j: _sigmoid(g_ref[j].astype(F32))
        parts.append(gate(c) * y_a[:, sl]
                     + gate(per_branch + c) * y_b[:, sl]
                     + gate(2 * per_branch + c) * y_c[:, sl])
    merged = jnp.concatenate(parts, axis=1).astype(BF16)
    x1 = x_ref[...] + jnp.dot(merged, wo_ref[...], preferred_element_type=F32)

    ms = jnp.mean(x1 * x1, axis=-1, keepdims=True)
    xn = (x1 * lax.rsqrt(ms + EPS) * fg_ref[...]).astype(BF16)
    fc = ffn_chunk
    d_ff = w2_ref.shape[0]
    y = x1
    for c in range(d_ff // fc):
        gt = jnp.dot(xn, w1_ref[:, fc * c:fc * (c + 1)], preferred_element_type=F32)
        up = jnp.dot(xn, w1_ref[:, d_ff + fc * c:d_ff + fc * (c + 1)],
                     preferred_element_type=F32)
        act = (gt * _sigmoid(gt) * up).astype(BF16)
        y = y + jnp.dot(act, w2_ref[fc * c:fc * (c + 1), :], preferred_element_type=F32)
    out_ref[...] = y


def _post(x2, o, hbn, hc, g, wa, wb, wc, wo, fg, w1, w2, *, tm, ffn_chunk):
    t, d = x2.shape
    n_gate_chunks = g.shape[0]
    d_ff = w2.shape[0]
    tile = lambda width: pl.BlockSpec((tm, width), lambda i: (i, 0))
    kern = functools.partial(_post_kernel, ffn_chunk=ffn_chunk)
    return pl.pallas_call(
        kern,
        out_shape=jax.ShapeDtypeStruct((t, d), F32),
        grid=(t // tm,),
        in_specs=[
            tile(d), tile(CHUNK), tile(CHUNK), tile(CHUNK),
            pl.BlockSpec((n_gate_chunks, tm, CHUNK), lambda i: (0, i, 0)),
            _const_spec((CHUNK, d)), _const_spec((CHUNK, d)), _const_spec((CHUNK, d)),
            _const_spec((d, d)), _const_spec((1, d)),
            _const_spec((d, 2 * d_ff)), _const_spec((d_ff, d)),
        ],
        out_specs=tile(d),
        compiler_params=_compiler_params(1),
        name="post",
    )(x2, o, hbn, hc, g, wa, wb, wc, wo, fg, w1, w2)


def _pick_tile(n, want):
    t = min(n, want)
    while n % t:
        t //= 2
    return t


def _ffn_chunk(d_ff):
    for c in (512, 256, 128):
        if d_ff % c == 0:
            return c
    raise ValueError(f"d_ff={d_ff} must be a multiple of {LANES}")


def kernel(x, mix_norm_g, w_in, q_norm_g, k_norm_g, w_attn_out, conf_dw_w, conf_dw_b,
           conf_ln_g, conf_ln_b, w_conf_out, sc_conv_w, w_sc_out, gate_b, w_o,
           ffn_norm_g, w_ffn_in, w_ffn_out):
    b, s, d = x.shape
    depth = w_in.shape[0]
    d_ff = w_ffn_out.shape[1]
    n_chunks = w_in.shape[2] // CHUNK
    assert w_in.shape[2] == N_BRANCH_CHUNKS * CHUNK + 3 * d and N_HEADS * HEAD_DIM == CHUNK
    assert d % CHUNK == 0 and s % (ATTN_ROWS * ATTN_Q_PAR) == 0 and s >= ATTN_WINDOW
    tm_in = _pick_tile(s, 512)
    tm_post = _pick_tile(b * s, 512)
    assert tm_in % (CONV_GROUPS * SUBLANES) == 0 and tm_in // SUBLANES >= _halo_groups(CONF_WIDTH)
    fc = _ffn_chunk(d_ff)

    head_id = np.arange(CHUNK) // HEAD_DIM
    hmean = jnp.asarray((head_id[:, None] == head_id[None, :]) / HEAD_DIM, dtype=BF16)
    scale = float(1.0 / np.sqrt(HEAD_DIM))
    sub_bcast = lambda w: jnp.broadcast_to(w[:, None, :], (w.shape[0], SUBLANES, w.shape[1]))

    x2 = x.reshape(b * s, d)
    for l in range(depth):
        qkvg = jnp.stack([jnp.tile(q_norm_g[l], N_HEADS) * scale,
                          jnp.tile(k_norm_g[l], N_HEADS),
                          jnp.ones((CHUNK,), F32)])[:, None, :]
        qkv, hbn, hc, g = _in_proj(
            x2, mix_norm_g[l][None, :], w_in[l].astype(BF16), hmean, qkvg,
            gate_b[l].reshape(n_chunks - N_BRANCH_CHUNKS, 1, CHUNK),
            sub_bcast(conf_dw_w[l]), conf_dw_b[l][None, :],
            conf_ln_g[l][None, :], conf_ln_b[l][None, :], sub_bcast(sc_conv_w[l]),
            tm=tm_in, tiles_per_seq=s // tm_in)
        o = _attention(qkv.reshape(3, b, s, CHUNK)).reshape(b * s, CHUNK)
        x2 = _post(
            x2, o, hbn, hc, g, w_attn_out[l].astype(BF16), w_conf_out[l].astype(BF16),
            w_sc_out[l].astype(BF16), w_o[l].astype(BF16), ffn_norm_g[l][None, :],
            w_ffn_in[l].astype(BF16), w_ffn_out[l].astype(BF16), tm=tm_post, ffn_chunk=fc)
    return x2.reshape(b, s, d)
```

```python
import functools

import jax
import jax.numpy as jnp
import numpy as np
from jax import lax
from jax.experimental import pallas as pl
from jax.experimental.pallas import tpu as pltpu

F32 = jnp.float32
BF16 = jnp.bfloat16

EPS = 1e-6
N_HEADS = 8
HEAD_DIM = 64
CONF_WIDTH = 31
SC_WIDTH = 3
LANES = 128
SUBLANES = 8
CHUNK = 512
N_BRANCH_CHUNKS = 8
CONV_GROUPS = 8
CONF_PRESHIFTS = 4
ATTN_ROWS = 128
ATTN_WINDOW = 256
ATTN_Q_PAR = 16
ATTN_FAR_ROWS = 64
UNDERFLOW_BOUND = 88.0
VMEM_LIMIT_BYTES = 56 * 1024 * 1024


def _compiler_params(n_grid_axes):
    return pltpu.CompilerParams(
        dimension_semantics=("arbitrary",) * n_grid_axes, vmem_limit_bytes=VMEM_LIMIT_BYTES)


def _const_spec(shape):
    nd = len(shape)
    return pl.BlockSpec(shape, lambda *_: (0,) * nd, pipeline_mode=pl.Buffered(1))


def _sigmoid(x):
    return 0.5 * jnp.tanh(0.5 * x) + 0.5


def _halo_groups(width):
    return -(-(width - 1) // SUBLANES)


def _shift_rows_down(x, s, carry_in):
    sub = lax.broadcasted_iota(jnp.int32, (1, SUBLANES, CHUNK), 1)
    r = pltpu.roll(x, s, axis=1)
    prev = jnp.concatenate([carry_in, r[:-1]], axis=0)
    return jnp.where(sub >= s, r, prev), r[-1:]


def _causal_conv_rows(srcs, w8_ref, width, g_out, n_groups):
    halo = _halo_groups(width)
    stride = SUBLANES // len(srcs)
    base = g_out + halo
    y = None
    for bp in range(min(stride, width)):
        lo = 0 if bp else 1
        u = None
        for k, src in enumerate(srcs):
            b = k * stride + bp
            for a in range((width - 1 - b) // SUBLANES + 1 if b < width else 0):
                tap = w8_ref[width - 1 - (SUBLANES * a + b)]
                t = tap * src[pl.ds(base - 1 + lo - a, n_groups + 1 - lo)]
                u = t if u is None else u + t
        if bp == 0:
            y = u
        else:
            y = y + _shift_rows_down(u[1:], bp, pltpu.roll(u[:1], bp, axis=1))[0]
    return y


def _in_proj_kernel(x_ref, w_ref, hmean_ref, cw_ref, scw_ref,
                    ng_ref, qkvg_ref, gb_ref, cb_ref, lng_ref, lnb_ref,
                    qkv_out, hbn_out, hc_out, g_out,
                    u_scr, hbe_scr, hbs_scr, me_scr, bg_scr, *, tiles_per_seq):
    tm = x_ref.shape[0]
    groups = tm // SUBLANES
    conf_halo = _halo_groups(CONF_WIDTH)
    sc_halo = _halo_groups(SC_WIDTH)
    n_gate_chunks = g_out.shape[0]
    conv_rows = CONV_GROUPS * SUBLANES
    n_conv = groups // CONV_GROUPS

    @pl.when(pl.program_id(0) % tiles_per_seq == 0)
    def _():
        hbe_scr[0:conf_halo] = jnp.zeros((conf_halo, SUBLANES, CHUNK), F32)
        me_scr[0:sc_halo] = jnp.zeros((sc_halo, SUBLANES, CHUNK), F32)

    x = x_ref[...]
    ms = jnp.mean(x * x, axis=-1, keepdims=True)
    u_scr[...] = (x * lax.rsqrt(ms + EPS) * ng_ref[...]).astype(BF16)

    def proj(c):
        return jnp.dot(u_scr[...], w_ref[:, c * CHUNK:(c + 1) * CHUNK],
                       preferred_element_type=F32)

    def grouped(a):
        return a.reshape(groups, SUBLANES, CHUNK)

    cf_val = proj(3)
    hbe_scr[conf_halo:] = grouped(cf_val * _sigmoid(proj(4)))

    def conf_step(c):
        rows = pl.ds(pl.multiple_of(c * conv_rows, conv_rows), conv_rows)
        srcs = [hbe_scr] + [hbs_scr.at[k] for k in range(CONF_PRESHIFTS - 1)]
        acc = _causal_conv_rows(srcs, cw_ref, CONF_WIDTH, c * CONV_GROUPS, CONV_GROUPS)
        acc = acc.reshape(conv_rows, CHUNK) + cb_ref[...]
        mu = jnp.mean(acc, axis=-1, keepdims=True)
        cen = acc - mu
        var = jnp.mean(cen * cen, axis=-1, keepdims=True)
        y = cen * lax.rsqrt(var + EPS) * lng_ref[...] + lnb_ref[...]
        hbn_out[rows, :] = (y * _sigmoid(y)).astype(BF16)

    def sc_step(c):
        rows = pl.ds(pl.multiple_of(c * conv_rows, conv_rows), conv_rows)
        sc = _causal_conv_rows([me_scr], scw_ref, SC_WIDTH, c * CONV_GROUPS, CONV_GROUPS)
        hc = bg_scr[pl.ds(c * CONV_GROUPS, CONV_GROUPS)] * sc
        hc_out[rows, :] = hc.reshape(conv_rows, CHUNK).astype(BF16)

    sc_x = proj(5)
    bg_scr[...] = grouped(proj(6))
    me_scr[sc_halo:] = grouped(proj(7) * sc_x)

    for j in range(n_gate_chunks):
        g_out[j] = (proj(N_BRANCH_CHUNKS + j) + gb_ref[j]).astype(BF16)
    for i in range(3):
        a = proj(i)
        if i < 2:
            msq = jnp.dot((a * a).astype(BF16), hmean_ref[...], preferred_element_type=F32)
            a = a * lax.rsqrt(msq + EPS)
        qkv_out[i] = (a * qkvg_ref[i]).astype(BF16)

    stride = SUBLANES // CONF_PRESHIFTS
    piece = 16
    carries = [jnp.zeros((1, SUBLANES, CHUNK), F32)] * (CONF_PRESHIFTS - 1)
    for g0 in range(0, groups + conf_halo, piece):
        n = min(piece, groups + conf_halo - g0)
        x_piece = hbe_scr[g0:g0 + n]
        for k in range(CONF_PRESHIFTS - 1):
            hbs_scr[k, g0:g0 + n], carries[k] = _shift_rows_down(
                x_piece, (k + 1) * stride, carries[k])

    steps_per_iter = 2 if n_conv % 2 == 0 else 1

    def conv_iter(i, carry):
        for s in range(steps_per_iter):
            conf_step(i * steps_per_iter + s)
            sc_step(i * steps_per_iter + s)
        return carry

    lax.fori_loop(0, n_conv // steps_per_iter, conv_iter, 0)

    hbe_scr[0:conf_halo] = hbe_scr[groups:groups + conf_halo]
    me_scr[0:sc_halo] = me_scr[groups:groups + sc_halo]


def _in_proj(x2, ng, w, hmean, qkvg, gb, cw8, cb, lng, lnb, scw8, *, tm, tiles_per_seq):
    t, d = x2.shape
    n_chunks = w.shape[1] // CHUNK
    n_gate_chunks = n_chunks - N_BRANCH_CHUNKS
    groups = tm // SUBLANES
    row = lambda i: (i, 0)
    stacked = lambda i: (0, i, 0)
    outs = [
        jax.ShapeDtypeStruct((3, t, CHUNK), BF16),
        jax.ShapeDtypeStruct((t, CHUNK), BF16),
        jax.ShapeDtypeStruct((t, CHUNK), BF16),
        jax.ShapeDtypeStruct((n_gate_chunks, t, CHUNK), BF16),
    ]
    out_specs = [
        pl.BlockSpec((3, tm, CHUNK), stacked),
        pl.BlockSpec((tm, CHUNK), row),
        pl.BlockSpec((tm, CHUNK), row),
        pl.BlockSpec((n_gate_chunks, tm, CHUNK), stacked),
    ]
    kern = functools.partial(_in_proj_kernel, tiles_per_seq=tiles_per_seq)
    return pl.pallas_call(
        kern,
        out_shape=outs,
        grid=(t // tm,),
        in_specs=[
            pl.BlockSpec((tm, d), row),
            _const_spec((d, n_chunks * CHUNK)),
            _const_spec((CHUNK, CHUNK)),
            _const_spec((CONF_WIDTH, SUBLANES, CHUNK)),
            _const_spec((SC_WIDTH, SUBLANES, CHUNK)),
            _const_spec((1, d)),
            _const_spec((3, 1, CHUNK)),
            _const_spec((n_gate_chunks, 1, CHUNK)),
            _const_spec((1, CHUNK)),
            _const_spec((1, CHUNK)),
            _const_spec((1, CHUNK)),
        ],
        out_specs=out_specs,
        scratch_shapes=[
            pltpu.VMEM((tm, d), BF16),
            pltpu.VMEM((groups + _halo_groups(CONF_WIDTH), SUBLANES, CHUNK), F32),
            pltpu.VMEM((CONF_PRESHIFTS - 1, groups + _halo_groups(CONF_WIDTH), SUBLANES, CHUNK), F32),
            pltpu.VMEM((groups + _halo_groups(SC_WIDTH), SUBLANES, CHUNK), F32),
            pltpu.VMEM((groups, SUBLANES, CHUNK), F32),
        ],
        compiler_params=_compiler_params(1),
        name="in_proj",
    )(x2, w, hmean, cw8, scw8, ng, qkvg, gb, cb, lng, lnb)


def _attn_kernel(q_ref, k_ref, v_ref, o_ref, c_scr, acc_scr):
    s_len = q_ref.shape[0]
    qr, win, top = ATTN_ROWS, ATTN_WINDOW, ATTN_FAR_ROWS
    n_q = s_len // qr
    lane = lax.broadcasted_iota(jnp.int32, (1, LANES), 1)
    head0 = lane < HEAD_DIM
    rows = lax.broadcasted_iota(jnp.int32, (win, win), 0)
    cols = lax.broadcasted_iota(jnp.int32, (win, win), 1)
    tri = (rows >= cols).astype(BF16)
    q_row = lax.broadcasted_iota(jnp.int32, (2 * qr, win), 0) & (qr - 1)
    k_col = lax.broadcasted_iota(jnp.int32, (2 * qr, win), 1)


    def softplus(z):
        neg_abs = lax.bitcast_convert_type(
            lax.bitcast_convert_type(z, jnp.uint32) | jnp.uint32(0x80000000), F32)
        return jnp.maximum(z, 0.0) + jnp.log(1.0 + jnp.exp(neg_abs))

    def rev_cumsum(sp, t):
        return jnp.dot(sp.astype(BF16), t, preferred_element_type=F32)

    def stacked_q(q0, nrows=qr):
        qb = q_ref[pl.ds(q0, nrows), :]
        zero = jnp.zeros((), BF16)
        return jnp.concatenate([jnp.where(head0, qb, zero), jnp.where(head0, zero, qb)], axis=0)

    def merge_heads(pv):
        half = pv.shape[0] // 2
        return jnp.where(head0, pv[:half], pv[half:])

    def split_min(c):
        lo = jnp.minimum(jnp.min(c[:top]), jnp.min(c[qr:qr + top]))
        hi = jnp.minimum(jnp.min(c[top:qr]), jnp.min(c[qr + top:]))
        return lo, hi

    def near_diagonal(j):
        zs, masks, sps, k0s = [], [], [], []
        for t in range(ATTN_Q_PAR):
            q0 = pl.multiple_of((j * ATTN_Q_PAR + t) * qr, qr)
            k0 = pl.multiple_of(jnp.maximum(q0 + qr - win, 0), qr)
            z = lax.dot_general(stacked_q(q0), k_ref[pl.ds(k0, win), :],
                                (((1,), (1,)), ((), ())), preferred_element_type=F32)
            causal = k_col < q_row + (q0 - k0)
            zs.append(z)
            masks.append(causal)
            sps.append(jnp.where(causal, softplus(z), 0.0))
            k0s.append(k0)
        cs_all = rev_cumsum(jnp.concatenate(sps, axis=0), tri)
        cmins = []
        for t in range(ATTN_Q_PAR):
            cs = cs_all[t * 2 * qr:(t + 1) * 2 * qr]
            p = jnp.where(masks[t], jnp.exp(zs[t] - cs), 0.0).astype(BF16)
            c = cs[:, 0:1]
            c_scr[t] = c
            cmins.append(split_min(c))
            pv = jnp.dot(p, v_ref[pl.ds(k0s[t], win), :], preferred_element_type=F32)
            acc_scr[t] = merge_heads(pv)
        return [m[0] for m in cmins], [m[1] for m in cmins]

    def q_group(j, carry):
        cm_top, cm_rest = near_diagonal(j)
        first = j * ATTN_Q_PAR
        q0s = [pl.multiple_of((first + t) * qr, qr) for t in range(ATTN_Q_PAR)]
        tri_far = tri[:qr, :qr]

        def next_key_block(t, step):
            return first + t - win // qr - step

        def pending(step, cm):
            todo = [jnp.logical_and(next_key_block(t, step) >= 0, cm[t] < UNDERFLOW_BOUND)
                    for t in range(ATTN_Q_PAR)]
            return functools.reduce(jnp.logical_or, todo)

        def far_step(step, nrows):
            zs, sps, kfs, valids = [], [], [], []
            for t in range(ATTN_Q_PAR):
                kb = next_key_block(t, step)
                kf = pl.multiple_of(jnp.maximum(kb, 0) * qr, qr)
                z = lax.dot_general(stacked_q(q0s[t], nrows), k_ref[pl.ds(kf, qr), :],
                                    (((1,), (1,)), ((), ())), preferred_element_type=F32)
                zs.append(z)
                sps.append(softplus(z))
                kfs.append(kf)
                valids.append(kb >= 0)
            cs_all = rev_cumsum(jnp.concatenate(sps, axis=0), tri_far)
            cs_new = []
            for t in range(ATTN_Q_PAR):
                cs = cs_all[t * 2 * nrows:(t + 1) * 2 * nrows]
                c_prev = jnp.concatenate([c_scr[t, 0:nrows], c_scr[t, qr:qr + nrows]], axis=0)
                p = jnp.where(valids[t], jnp.exp(zs[t] - cs - c_prev), 0.0).astype(BF16)
                c = c_prev + jnp.where(valids[t], cs[:, 0:1], 0.0)
                c_scr[t, 0:nrows] = c[:nrows]
                c_scr[t, qr:qr + nrows] = c[nrows:]
                acc_scr[t, 0:nrows] += merge_heads(
                    jnp.dot(p, v_ref[pl.ds(kfs[t], qr), :], preferred_element_type=F32))
                cs_new.append(c)
            return cs_new

        def whole_cond(st):
            return pending(st[0], st[1 + ATTN_Q_PAR:])

        def whole_body(st):
            mins = [split_min(c) for c in far_step(st[0], qr)]
            return (st[0] + 1, *[m[0] for m in mins], *[m[1] for m in mins])

        st = lax.while_loop(whole_cond, whole_body, (jnp.int32(0), *cm_top, *cm_rest))

        def top_cond(st):
            return pending(st[0], st[1:])

        def top_body(st):
            return (st[0] + 1, *[jnp.min(c) for c in far_step(st[0], top)])

        lax.while_loop(top_cond, top_body, st[:1 + ATTN_Q_PAR])
        for t in range(ATTN_Q_PAR):
            o_ref[pl.ds(q0s[t], qr), :] = acc_scr[t].astype(o_ref.dtype)
        return carry

    lax.fori_loop(0, n_q // ATTN_Q_PAR, q_group, 0)


def _attention(qkv):
    _, b, s, w = qkv.shape
    n_pairs = w // LANES
    spec = lambda which: pl.BlockSpec((None, None, s, LANES), lambda bi, hp: (which, bi, 0, hp))
    return pl.pallas_call(
        _attn_kernel,
        out_shape=jax.ShapeDtypeStruct((b, s, w), BF16),
        grid=(b, n_pairs),
        in_specs=[spec(0), spec(1), spec(2)],
        out_specs=pl.BlockSpec((None, s, LANES), lambda bi, hp: (bi, 0, hp)),
        scratch_shapes=[
            pltpu.VMEM((ATTN_Q_PAR, 2 * ATTN_ROWS, 1), F32),
            pltpu.VMEM((ATTN_Q_PAR, ATTN_ROWS, LANES), F32),
        ],
        compiler_params=_compiler_params(2),
        name="stickbreak_attn",
    )(qkv, qkv, qkv)


def _post_kernel(x_ref, o_ref, hbn_ref, hc_ref, g_ref,
                 wa_ref, wb_ref, wc_ref, wo_ref, fg_ref, w1_ref, w2_ref,
                 out_ref, *, ffn_chunk):
    d = x_ref.shape[1]
    per_branch = d // CHUNK
    y_a = jnp.dot(o_ref[...], wa_ref[...], preferred_element_type=F32)
    y_b = jnp.dot(hbn_ref[...], wb_ref[...], preferred_element_type=F32)
    y_c = jnp.dot(hc_ref[...], wc_ref[...], preferred_element_type=F32)
    parts = []
    for c in range(per_branch):
        sl = slice(c * CHUNK, (c + 1) * CHUNK)
        gate = lambda j: _sigmoid(g_ref[j].astype(F32))
        parts.append(gate(c) * y_a[:, sl]
                     + gate(per_branch + c) * y_b[:, sl]
                     + gate(2 * per_branch + c) * y_c[:, sl])
    merged = jnp.concatenate(parts, axis=1).astype(BF16)
    x1 = x_ref[...] + jnp.dot(merged, wo_ref[...], preferred_element_type=F32)

    ms = jnp.mean(x1 * x1, axis=-1, keepdims=True)
    xn = (x1 * lax.rsqrt(ms + EPS) * fg_ref[...]).astype(BF16)
    fc = ffn_chunk
    d_ff = w2_ref.shape[0]
    y = x1
    for c in range(d_ff // fc):
        gt = jnp.dot(xn, w1_ref[:, fc * c:fc * (c + 1)], preferred_element_type=F32)
        up = jnp.dot(xn, w1_ref[:, d_ff + fc * c:d_ff + fc * (c + 1)],
                     preferred_element_type=F32)
        act = (gt * _sigmoid(gt) * up).astype(BF16)
        y = y + jnp.dot(act, w2_ref[fc * c:fc * (c + 1), :], preferred_element_type=F32)
    out_ref[...] = y


def _post(x2, o, hbn, hc, g, wa, wb, wc, wo, fg, w1, w2, *, tm, ffn_chunk):
    t, d = x2.shape
    n_gate_chunks = g.shape[0]
    d_ff = w2.shape[0]
    tile = lambda width: pl.BlockSpec((tm, width), lambda i: (i, 0))
    kern = functools.partial(_post_kernel, ffn_chunk=ffn_chunk)
    return pl.pallas_call(
        kern,
        out_shape=jax.ShapeDtypeStruct((t, d), F32),
        grid=(t // tm,),
        in_specs=[
            tile(d), tile(CHUNK), tile(CHUNK), tile(CHUNK),
            pl.BlockSpec((n_gate_chunks, tm, CHUNK), lambda i: (0, i, 0)),
            _const_spec((CHUNK, d)), _const_spec((CHUNK, d)), _const_spec((CHUNK, d)),
            _const_spec((d, d)), _const_spec((1, d)),
            _const_spec((d, 2 * d_ff)), _const_spec((d_ff, d)),
        ],
        out_specs=tile(d),
        compiler_params=_compiler_params(1),
        name="post",
    )(x2, o, hbn, hc, g, wa, wb, wc, wo, fg, w1, w2)


def _pick_tile(n, want):
    t = min(n, want)
    while n % t:
        t //= 2
    return t


def _ffn_chunk(d_ff):
    for c in (512, 256, 128):
        if d_ff % c == 0:
            return c
    raise ValueError(f"d_ff={d_ff} must be a multiple of {LANES}")


def kernel(x, mix_norm_g, w_in, q_norm_g, k_norm_g, w_attn_out, conf_dw_w, conf_dw_b,
           conf_ln_g, conf_ln_b, w_conf_out, sc_conv_w, w_sc_out, gate_b, w_o,
           ffn_norm_g, w_ffn_in, w_ffn_out):
    b, s, d = x.shape
    depth = w_in.shape[0]
    d_ff = w_ffn_out.shape[1]
    n_chunks = w_in.shape[2] // CHUNK
    assert w_in.shape[2] == N_BRANCH_CHUNKS * CHUNK + 3 * d and N_HEADS * HEAD_DIM == CHUNK
    assert d % CHUNK == 0 and s % (ATTN_ROWS * ATTN_Q_PAR) == 0 and s >= ATTN_WINDOW
    tm_in = _pick_tile(s, 512)
    tm_post = _pick_tile(b * s, 512)
    assert tm_in % (CONV_GROUPS * SUBLANES) == 0 and tm_in // SUBLANES >= _halo_groups(CONF_WIDTH)
    fc = _ffn_chunk(d_ff)

    head_id = np.arange(CHUNK) // HEAD_DIM
    hmean = jnp.asarray((head_id[:, None] == head_id[None, :]) / HEAD_DIM, dtype=BF16)
    scale = float(1.0 / np.sqrt(HEAD_DIM))
    sub_bcast = lambda w: jnp.broadcast_to(w[:, None, :], (w.shape[0], SUBLANES, w.shape[1]))

    x2 = x.reshape(b * s, d)
    for l in range(depth):
        qkvg = jnp.stack([jnp.tile(q_norm_g[l], N_HEADS) * scale,
                          jnp.tile(k_norm_g[l], N_HEADS),
                          jnp.ones((CHUNK,), F32)])[:, None, :]
        qkv, hbn, hc, g = _in_proj(
            x2, mix_norm_g[l][None, :], w_in[l].astype(BF16), hmean, qkvg,
            gate_b[l].reshape(n_chunks - N_BRANCH_CHUNKS, 1, CHUNK),
            sub_bcast(conf_dw_w[l]), conf_dw_b[l][None, :],
            conf_ln_g[l][None, :], conf_ln_b[l][None, :], sub_bcast(sc_conv_w[l]),
            tm=tm_in, tiles_per_seq=s // tm_in)
        o = _attention(qkv.reshape(3, b, s, CHUNK)).reshape(b * s, CHUNK)
        x2 = _post(
            x2, o, hbn, hc, g, w_attn_out[l].astype(BF16), w_conf_out[l].astype(BF16),
            w_sc_out[l].astype(BF16), w_o[l].astype(BF16), ffn_norm_g[l][None, :],
            w_ffn_in[l].astype(BF16), w_ffn_out[l].astype(BF16), tm=tm_post, ffn_chunk=fc)
    return x2.reshape(b, s, d)
```

```python
import functools

import jax
import jax.numpy as jnp
import numpy as np
from jax import lax
from jax.experimental import pallas as pl
from jax.experimental.pallas import tpu as pltpu

F32 = jnp.float32
BF16 = jnp.bfloat16

EPS = 1e-6
N_HEADS = 8
HEAD_DIM = 64
CONF_WIDTH = 31
SC_WIDTH = 3
LANES = 128
SUBLANES = 8
CHUNK = 512
N_BRANCH_CHUNKS = 8
CONV_GROUPS = 8
CONF_PRESHIFTS = 4
ATTN_ROWS = 128
ATTN_WINDOW = 256
ATTN_Q_PAR = 16
ATTN_FAR_ROWS = 64
UNDERFLOW_BOUND = 88.0
VMEM_LIMIT_BYTES = 56 * 1024 * 1024


def _compiler_params(n_grid_axes):
    return pltpu.CompilerParams(
        dimension_semantics=("arbitrary",) * n_grid_axes, vmem_limit_bytes=VMEM_LIMIT_BYTES)


def _const_spec(shape):
    nd = len(shape)
    return pl.BlockSpec(shape, lambda *_: (0,) * nd, pipeline_mode=pl.Buffered(1))


def _sigmoid(x):
    return 0.5 * jnp.tanh(0.5 * x) + 0.5


def _halo_groups(width):
    return -(-(width - 1) // SUBLANES)


def _shift_rows_down(x, s, carry_in):
    sub = lax.broadcasted_iota(jnp.int32, (1, SUBLANES, CHUNK), 1)
    r = pltpu.roll(x, s, axis=1)
    prev = jnp.concatenate([carry_in, r[:-1]], axis=0)
    return jnp.where(sub >= s, r, prev), r[-1:]


def _causal_conv_rows(srcs, w8_ref, width, g_out, n_groups):
    halo = _halo_groups(width)
    stride = SUBLANES // len(srcs)
    base = g_out + halo
    y = None
    for bp in range(min(stride, width)):
        lo = 0 if bp else 1
        u = None
        for k, src in enumerate(srcs):
            b = k * stride + bp
            for a in range((width - 1 - b) // SUBLANES + 1 if b < width else 0):
                tap = w8_ref[width - 1 - (SUBLANES * a + b)]
                t = tap * src[pl.ds(base - 1 + lo - a, n_groups + 1 - lo)]
                u = t if u is None else u + t
        if bp == 0:
            y = u
        else:
            y = y + _shift_rows_down(u[1:], bp, pltpu.roll(u[:1], bp, axis=1))[0]
    return y


def _in_proj_kernel(x_ref, w_ref, hmean_ref, cw_ref, scw_ref,
                    ng_ref, qkvg_ref, gb_ref, cb_ref, lng_ref, lnb_ref,
                    qkv_out, hbn_out, hc_out, g_out,
                    u_scr, hbe_scr, hbs_scr, me_scr, bg_scr, *, tiles_per_seq):
    tm = x_ref.shape[0]
    groups = tm // SUBLANES
    conf_halo = _halo_groups(CONF_WIDTH)
    sc_halo = _halo_groups(SC_WIDTH)
    n_gate_chunks = g_out.shape[0]
    conv_rows = CONV_GROUPS * SUBLANES
    n_conv = groups // CONV_GROUPS

    @pl.when(pl.program_id(0) % tiles_per_seq == 0)
    def _():
        hbe_scr[0:conf_halo] = jnp.zeros((conf_halo, SUBLANES, CHUNK), F32)
        me_scr[0:sc_halo] = jnp.zeros((sc_halo, SUBLANES, CHUNK), F32)

    x = x_ref[...]
    ms = jnp.mean(x * x, axis=-1, keepdims=True)
    u_scr[...] = (x * lax.rsqrt(ms + EPS) * ng_ref[...]).astype(BF16)

    def proj(c):
        return jnp.dot(u_scr[...], w_ref[:, c * CHUNK:(c + 1) * CHUNK],
                       preferred_element_type=F32)

    def grouped(a):
        return a.reshape(groups, SUBLANES, CHUNK)

    cf_val = proj(3)
    hbe_scr[conf_halo:] = grouped(cf_val * _sigmoid(proj(4)))

    def conf_step(c):
        rows = pl.ds(pl.multiple_of(c * conv_rows, conv_rows), conv_rows)
        srcs = [hbe_scr] + [hbs_scr.at[k] for k in range(CONF_PRESHIFTS - 1)]
        acc = _causal_conv_rows(srcs, cw_ref, CONF_WIDTH, c * CONV_GROUPS, CONV_GROUPS)
        acc = acc.reshape(conv_rows, CHUNK) + cb_ref[...]
        mu = jnp.mean(acc, axis=-1, keepdims=True)
        cen = acc - mu
        var = jnp.mean(cen * cen, axis=-1, keepdims=True)
        y = cen * lax.rsqrt(var + EPS) * lng_ref[...] + lnb_ref[...]
        hbn_out[rows, :] = (y * _sigmoid(y)).astype(BF16)

    def sc_step(c):
        rows = pl.ds(pl.multiple_of(c * conv_rows, conv_rows), conv_rows)
        sc = _causal_conv_rows([me_scr], scw_ref, SC_WIDTH, c * CONV_GROUPS, CONV_GROUPS)
        hc = bg_scr[pl.ds(c * CONV_GROUPS, CONV_GROUPS)] * sc
        hc_out[rows, :] = hc.reshape(conv_rows, CHUNK).astype(BF16)

    sc_x = proj(5)
    bg_scr[...] = grouped(proj(6))
    me_scr[sc_halo:] = grouped(proj(7) * sc_x)

    for j in range(n_gate_chunks):
        g_out[j] = (proj(N_BRANCH_CHUNKS + j) + gb_ref[j]).astype(BF16)
    for i in range(3):
        a = proj(i)
        if i < 2:
            msq = jnp.dot((a * a).astype(BF16), hmean_ref[...], preferred_element_type=F32)
            a = a * lax.rsqrt(msq + EPS)
        qkv_out[i] = (a * qkvg_ref[i]).astype(BF16)

    stride = SUBLANES // CONF_PRESHIFTS
    piece = 16
    carries = [jnp.zeros((1, SUBLANES, CHUNK), F32)] * (CONF_PRESHIFTS - 1)
    for g0 in range(0, groups + conf_halo, piece):
        n = min(piece, groups + conf_halo - g0)
        x_piece = hbe_scr[g0:g0 + n]
        for k in range(CONF_PRESHIFTS - 1):
            hbs_scr[k, g0:g0 + n], carries[k] = _shift_rows_down(
                x_piece, (k + 1) * stride, carries[k])

    steps_per_iter = 4 if n_conv % 4 == 0 else 1

    def conv_iter(i, carry):
        for s in range(steps_per_iter):
            conf_step(i * steps_per_iter + s)
            sc_step(i * steps_per_iter + s)
        return carry

    lax.fori_loop(0, n_conv // steps_per_iter, conv_iter, 0)

    hbe_scr[0:conf_halo] = hbe_scr[groups:groups + conf_halo]
    me_scr[0:sc_halo] = me_scr[groups:groups + sc_halo]


def _in_proj(x2, ng, w, hmean, qkvg, gb, cw8, cb, lng, lnb, scw8, *, tm, tiles_per_seq):
    t, d = x2.shape
    n_chunks = w.shape[1] // CHUNK
    n_gate_chunks = n_chunks - N_BRANCH_CHUNKS
    groups = tm // SUBLANES
    row = lambda i: (i, 0)
    stacked = lambda i: (0, i, 0)
    outs = [
        jax.ShapeDtypeStruct((3, t, CHUNK), BF16),
        jax.ShapeDtypeStruct((t, CHUNK), BF16),
        jax.ShapeDtypeStruct((t, CHUNK), BF16),
        jax.ShapeDtypeStruct((n_gate_chunks, t, CHUNK), BF16),
    ]
    out_specs = [
        pl.BlockSpec((3, tm, CHUNK), stacked),
        pl.BlockSpec((tm, CHUNK), row),
        pl.BlockSpec((tm, CHUNK), row),
        pl.BlockSpec((n_gate_chunks, tm, CHUNK), stacked),
    ]
    kern = functools.partial(_in_proj_kernel, tiles_per_seq=tiles_per_seq)
    return pl.pallas_call(
        kern,
        out_shape=outs,
        grid=(t // tm,),
        in_specs=[
            pl.BlockSpec((tm, d), row),
            _const_spec((d, n_chunks * CHUNK)),
            _const_spec((CHUNK, CHUNK)),
            _const_spec((CONF_WIDTH, SUBLANES, CHUNK)),
            _const_spec((SC_WIDTH, SUBLANES, CHUNK)),
            _const_spec((1, d)),
            _const_spec((3, 1, CHUNK)),
            _const_spec((n_gate_chunks, 1, CHUNK)),
            _const_spec((1, CHUNK)),
            _const_spec((1, CHUNK)),
            _const_spec((1, CHUNK)),
        ],
        out_specs=out_specs,
        scratch_shapes=[
            pltpu.VMEM((tm, d), BF16),
            pltpu.VMEM((groups + _halo_groups(CONF_WIDTH), SUBLANES, CHUNK), F32),
            pltpu.VMEM((CONF_PRESHIFTS - 1, groups + _halo_groups(CONF_WIDTH), SUBLANES, CHUNK), F32),
            pltpu.VMEM((groups + _halo_groups(SC_WIDTH), SUBLANES, CHUNK), F32),
            pltpu.VMEM((groups, SUBLANES, CHUNK), F32),
        ],
        compiler_params=_compiler_params(1),
        name="in_proj",
    )(x2, w, hmean, cw8, scw8, ng, qkvg, gb, cb, lng, lnb)


def _attn_kernel(q_ref, k_ref, v_ref, o_ref, c_scr, acc_scr):
    s_len = q_ref.shape[0]
    qr, win, top = ATTN_ROWS, ATTN_WINDOW, ATTN_FAR_ROWS
    n_q = s_len // qr
    lane = lax.broadcasted_iota(jnp.int32, (1, LANES), 1)
    head0 = lane < HEAD_DIM
    rows = lax.broadcasted_iota(jnp.int32, (win, win), 0)
    cols = lax.broadcasted_iota(jnp.int32, (win, win), 1)
    tri = (rows >= cols).astype(BF16)
    q_row = lax.broadcasted_iota(jnp.int32, (2 * qr, win), 0) & (qr - 1)
    k_col = lax.broadcasted_iota(jnp.int32, (2 * qr, win), 1)


    def softplus(z):
        neg_abs = lax.bitcast_convert_type(
            lax.bitcast_convert_type(z, jnp.uint32) | jnp.uint32(0x80000000), F32)
        return jnp.maximum(z, 0.0) + jnp.log(1.0 + jnp.exp(neg_abs))

    def rev_cumsum(sp, t):
        return jnp.dot(sp.astype(BF16), t, preferred_element_type=F32)

    def stacked_q(q0, nrows=qr):
        qb = q_ref[pl.ds(q0, nrows), :]
        zero = jnp.zeros((), BF16)
        return jnp.concatenate([jnp.where(head0, qb, zero), jnp.where(head0, zero, qb)], axis=0)

    def merge_heads(pv):
        half = pv.shape[0] // 2
        return jnp.where(head0, pv[:half], pv[half:])

    def split_min(c):
        lo = jnp.minimum(jnp.min(c[:top]), jnp.min(c[qr:qr + top]))
        hi = jnp.minimum(jnp.min(c[top:qr]), jnp.min(c[qr + top:]))
        return lo, hi

    def near_diagonal(j):
        zs, masks, sps, k0s = [], [], [], []
        for t in range(ATTN_Q_PAR):
            q0 = pl.multiple_of((j * ATTN_Q_PAR + t) * qr, qr)
            k0 = pl.multiple_of(jnp.maximum(q0 + qr - win, 0), qr)
            z = lax.dot_general(stacked_q(q0), k_ref[pl.ds(k0, win), :],
                                (((1,), (1,)), ((), ())), preferred_element_type=F32)
            causal = k_col < q_row + (q0 - k0)
            zs.append(z)
            masks.append(causal)
            sps.append(jnp.where(causal, softplus(z), 0.0))
            k0s.append(k0)
        cs_all = rev_cumsum(jnp.concatenate(sps, axis=0), tri)
        cmins = []
        for t in range(ATTN_Q_PAR):
            cs = cs_all[t * 2 * qr:(t + 1) * 2 * qr]
            p = jnp.where(masks[t], jnp.exp(zs[t] - cs), 0.0).astype(BF16)
            c = cs[:, 0:1]
            c_scr[t] = c
            cmins.append(split_min(c))
            pv = jnp.dot(p, v_ref[pl.ds(k0s[t], win), :], preferred_element_type=F32)
            acc_scr[t] = merge_heads(pv)
        return [m[0] for m in cmins], [m[1] for m in cmins]

    def q_group(j, carry):
        cm_top, cm_rest = near_diagonal(j)
        first = j * ATTN_Q_PAR
        q0s = [pl.multiple_of((first + t) * qr, qr) for t in range(ATTN_Q_PAR)]
        tri_far = tri[:qr, :qr]

        def next_key_block(t, step):
            return first + t - win // qr - step

        def pending(step, cm):
            todo = [jnp.logical_and(next_key_block(t, step) >= 0, cm[t] < UNDERFLOW_BOUND)
                    for t in range(ATTN_Q_PAR)]
            return functools.reduce(jnp.logical_or, todo)

        def far_step(step, nrows):
            zs, sps, kfs, valids = [], [], [], []
            for t in range(ATTN_Q_PAR):
                kb = next_key_block(t, step)
                kf = pl.multiple_of(jnp.maximum(kb, 0) * qr, qr)
                z = lax.dot_general(stacked_q(q0s[t], nrows), k_ref[pl.ds(kf, qr), :],
                                    (((1,), (1,)), ((), ())), preferred_element_type=F32)
                zs.append(z)
                sps.append(softplus(z))
                kfs.append(kf)
                valids.append(kb >= 0)
            cs_all = rev_cumsum(jnp.concatenate(sps, axis=0), tri_far)
            cs_new = []
            for t in range(ATTN_Q_PAR):
                cs = cs_all[t * 2 * nrows:(t + 1) * 2 * nrows]
                c_prev = jnp.concatenate([c_scr[t, 0:nrows], c_scr[t, qr:qr + nrows]], axis=0)
                p = jnp.where(valids[t], jnp.exp(zs[t] - cs - c_prev), 0.0).astype(BF16)
                c = c_prev + jnp.where(valids[t], cs[:, 0:1], 0.0)
                c_scr[t, 0:nrows] = c[:nrows]
                c_scr[t, qr:qr + nrows] = c[nrows:]
                acc_scr[t, 0:nrows] += merge_heads(
                    jnp.dot(p, v_ref[pl.ds(kfs[t], qr), :], preferred_element_type=F32))
                cs_new.append(c)
            return cs_new

        def whole_cond(st):
            return pending(st[0], st[1 + ATTN_Q_PAR:])

        def whole_body(st):
            mins = [split_min(c) for c in far_step(st[0], qr)]
            return (st[0] + 1, *[m[0] for m in mins], *[m[1] for m in mins])

        st = lax.while_loop(whole_cond, whole_body, (jnp.int32(0), *cm_top, *cm_rest))

        def top_cond(st):
            return pending(st[0], st[1:])

        def top_body(st):
            return (st[0] + 1, *[jnp.min(c) for c in far_step(st[0], top)])

        lax.while_loop(top_cond, top_body, st[:1 + ATTN_Q_PAR])
        for t in range(ATTN_Q_PAR):
            o_ref[pl.ds(q0s[t], qr), :] = acc_scr[t].astype(o_ref.dtype)
        return carry

    lax.fori_loop(0, n_q // ATTN_Q_PAR, q_group, 0)


def _attention(qkv):
    _, b, s, w = qkv.shape
    n_pairs = w // LANES
    spec = lambda which: pl.BlockSpec((None, None, s, LANES), lambda bi, hp: (which, bi, 0, hp))
    return pl.pallas_call(
        _attn_kernel,
        out_shape=jax.ShapeDtypeStruct((b, s, w), BF16),
        grid=(b, n_pairs),
        in_specs=[spec(0), spec(1), spec(2)],
        out_specs=pl.BlockSpec((None, s, LANES), lambda bi, hp: (bi, 0, hp)),
        scratch_shapes=[
            pltpu.VMEM((ATTN_Q_PAR, 2 * ATTN_ROWS, 1), F32),
            pltpu.VMEM((ATTN_Q_PAR, ATTN_ROWS, LANES), F32),
        ],
        compiler_params=_compiler_params(2),
        name="stickbreak_attn",
    )(qkv, qkv, qkv)


def _post_kernel(x_ref, o_ref, hbn_ref, hc_ref, g_ref,
                 wa_ref, wb_ref, wc_ref, wo_ref, fg_ref, w1_ref, w2_ref,
                 out_ref, *, ffn_chunk):
    d = x_ref.shape[1]
    per_branch = d // CHUNK
    y_a = jnp.dot(o_ref[...], wa_ref[...], preferred_element_type=F32)
    y_b = jnp.dot(hbn_ref[...], wb_ref[...], preferred_element_type=F32)
    y_c = jnp.dot(hc_ref[...], wc_ref[...], preferred_element_type=F32)
    parts = []
    for c in range(per_branch):
        sl = slice(c * CHUNK, (c + 1) * CHUNK)
        gate = lambda j: _sigmoid(g_ref[j].astype(F32))
        parts.append(gate(c) * y_a[:, sl]
                     + gate(per_branch + c) * y_b[:, sl]
                     + gate(2 * per_branch + c) * y_c[:, sl])
    merged = jnp.concatenate(parts, axis=1).astype(BF16)
    x1 = x_ref[...] + jnp.dot(merged, wo_ref[...], preferred_element_type=F32)

    ms = jnp.mean(x1 * x1, axis=-1, keepdims=True)
    xn = (x1 * lax.rsqrt(ms + EPS) * fg_ref[...]).astype(BF16)
    fc = ffn_chunk
    d_ff = w2_ref.shape[0]
    y = x1
    for c in range(d_ff // fc):
        gt = jnp.dot(xn, w1_ref[:, fc * c:fc * (c + 1)], preferred_element_type=F32)
        up = jnp.dot(xn, w1_ref[:, d_ff + fc * c:d_ff + fc * (c + 1)],
                     preferred_element_type=F32)
        act = (gt * _sigmoid(gt) * up).astype(BF16)
        y = y + jnp.dot(act, w2_ref[fc * c:fc * (c + 1), :], preferred_element_type=F32)
    out_ref[...] = y


def _post(x2, o, hbn, hc, g, wa, wb, wc, wo, fg, w1, w2, *, tm, ffn_chunk):
    t, d = x2.shape
    n_gate_chunks = g.shape[0]
    d_ff = w2.shape[0]
    tile = lambda width: pl.BlockSpec((tm, width), lambda i: (i, 0))
    kern = functools.partial(_post_kernel, ffn_chunk=ffn_chunk)
    return pl.pallas_call(
        kern,
        out_shape=jax.ShapeDtypeStruct((t, d), F32),
        grid=(t // tm,),
        in_specs=[
            tile(d), tile(CHUNK), tile(CHUNK), tile(CHUNK),
            pl.BlockSpec((n_gate_chunks, tm, CHUNK), lambda i: (0, i, 0)),
            _const_spec((CHUNK, d)), _const_spec((CHUNK, d)), _const_spec((CHUNK, d)),
            _const_spec((d, d)), _const_spec((1, d)),
            _const_spec((d, 2 * d_ff)), _const_spec((d_ff, d)),
        ],
        out_specs=tile(d),
        compiler_params=_compiler_params(1),
        name="post",
    )(x2, o, hbn, hc, g, wa, wb, wc, wo, fg, w1, w2)


def _pick_tile(n, want):
    t = min(n, want)
    while n % t:
        t //= 2
    return t


def _ffn_chunk(d_ff):
    for c in (512, 256, 128):
        if d_ff % c == 0:
            return c
    raise ValueError(f"d_ff={d_ff} must be a multiple of {LANES}")


def kernel(x, mix_norm_g, w_in, q_norm_g, k_norm_g, w_attn_out, conf_dw_w, conf_dw_b,
           conf_ln_g, conf_ln_b, w_conf_out, sc_conv_w, w_sc_out, gate_b, w_o,
           ffn_norm_g, w_ffn_in, w_ffn_out):
    b, s, d = x.shape
    depth = w_in.shape[0]
    d_ff = w_ffn_out.shape[1]
    n_chunks = w_in.shape[2] // CHUNK
    assert w_in.shape[2] == N_BRANCH_CHUNKS * CHUNK + 3 * d and N_HEADS * HEAD_DIM == CHUNK
    assert d % CHUNK == 0 and s % (ATTN_ROWS * ATTN_Q_PAR) == 0 and s >= ATTN_WINDOW
    tm_in = _pick_tile(s, 512)
    tm_post = _pick_tile(b * s, 512)
    assert tm_in % (CONV_GROUPS * SUBLANES) == 0 and tm_in // SUBLANES >= _halo_groups(CONF_WIDTH)
    fc = _ffn_chunk(d_ff)

    head_id = np.arange(CHUNK) // HEAD_DIM
    hmean = jnp.asarray((head_id[:, None] == head_id[None, :]) / HEAD_DIM, dtype=BF16)
    scale = float(1.0 / np.sqrt(HEAD_DIM))
    sub_bcast = lambda w: jnp.broadcast_to(w[:, None, :], (w.shape[0], SUBLANES, w.shape[1]))

    x2 = x.reshape(b * s, d)
    for l in range(depth):
        qkvg = jnp.stack([jnp.tile(q_norm_g[l], N_HEADS) * scale,
                          jnp.tile(k_norm_g[l], N_HEADS),
                          jnp.ones((CHUNK,), F32)])[:, None, :]
        qkv, hbn, hc, g = _in_proj(
            x2, mix_norm_g[l][None, :], w_in[l].astype(BF16), hmean, qkvg,
            gate_b[l].reshape(n_chunks - N_BRANCH_CHUNKS, 1, CHUNK),
            sub_bcast(conf_dw_w[l]), conf_dw_b[l][None, :],
            conf_ln_g[l][None, :], conf_ln_b[l][None, :], sub_bcast(sc_conv_w[l]),
            tm=tm_in, tiles_per_seq=s // tm_in)
        o = _attention(qkv.reshape(3, b, s, CHUNK)).reshape(b * s, CHUNK)
        x2 = _post(
            x2, o, hbn, hc, g, w_attn_out[l].astype(BF16), w_conf_out[l].astype(BF16),
            w_sc_out[l].astype(BF16), w_o[l].astype(BF16), ffn_norm_g[l][None, :],
            w_ffn_in[l].astype(BF16), w_ffn_out[l].astype(BF16), tm=tm_post, ffn_chunk=fc)
    return x2.reshape(b, s, d)
```
